```python
import jax, jax.numpy as jnp
from jax import lax
import numpy as np

D_MODEL = 2048
BATCH = 4
SEQ = 4096
DEPTH = 4

HEAD_DIM = 128
N_MIX_HEADS = D_MODEL // HEAD_DIM
N_HEADS_A = 4
N_HEADS_B = 6
N_HEADS_C = 6
DILATED_PATTERNS = ((128, 1), (512, 4), (2048, 16))
C_HEADS_PER_PATTERN = N_HEADS_C // len(DILATED_PATTERNS)
OUT_WIDTH = (N_HEADS_A + N_HEADS_B + C_HEADS_PER_PATTERN) * HEAD_DIM
MOBA_BLOCK = 256
MOBA_TOPK = 3
MOBA_Q_CHUNK = 64
SB_Q_BLOCK = 128
ROPE_THETA = 10000.0
MEM_LEN = 256
CROSS_HEADS = 4
CROSS_HEAD_DIM = 128
D_FF = 4 * D_MODEL
RMS_EPS = 1e-6
NEG_INF = -1e30

kernel_name = "hybrid_moba_stickbreak_dilated_block"


def rms_norm(x, g):
    x32 = x.astype(jnp.float32)
    y = x32 * lax.rsqrt(jnp.mean(x32 * x32, axis=-1, keepdims=True) + RMS_EPS)
    return (y * g.astype(jnp.float32)).astype(x.dtype)


def rope_tables(seq_len):
    inv_freq = 1.0 / (ROPE_THETA ** (jnp.arange(0, HEAD_DIM, 2, dtype=jnp.float32) / HEAD_DIM))
    ang = jnp.arange(seq_len, dtype=jnp.float32)[:, None] * inv_freq[None, :]
    return jnp.cos(ang), jnp.sin(ang)


def apply_rope(x, cos, sin):
    x32 = x.astype(jnp.float32)
    x1, x2 = jnp.split(x32, 2, axis=-1)
    return jnp.concatenate([x1 * cos - x2 * sin, x2 * cos + x1 * sin], axis=-1).astype(x.dtype)


def heads_to_channels(o):
    b, h, s, d = o.shape
    return jnp.transpose(o, (0, 2, 1, 3)).reshape(b, s, h * d)


def moba_attention(q, k, v):
    bsz, nh, seq, dh = q.shape
    scale = dh ** -0.5
    n_blk = -(-seq // MOBA_BLOCK)
    pad = ((0, 0), (0, 0), (0, n_blk * MOBA_BLOCK - seq), (0, 0))
    k_blk = jnp.pad(k, pad).reshape(bsz, nh, n_blk, MOBA_BLOCK, dh)
    v_blk = jnp.pad(v, pad).reshape(bsz, nh, n_blk, MOBA_BLOCK, dh)
    k_mean = jnp.mean(k_blk.astype(jnp.float32), axis=3)
    n_sel = min(MOBA_TOPK, n_blk - 1)
    n_chunk = seq // MOBA_Q_CHUNK
    q_chunks = jnp.moveaxis(q.reshape(bsz, nh, n_chunk, MOBA_Q_CHUNK, dh), 2, 0)
    starts = jnp.arange(n_chunk, dtype=jnp.int32) * MOBA_Q_CHUNK
    b_idx = jnp.arange(bsz)[:, None, None, None]
    h_idx = jnp.arange(nh)[None, :, None, None]
    blk_ids = jnp.arange(n_blk)
    in_blk = jnp.arange(MOBA_BLOCK)
    q_off = jnp.arange(MOBA_Q_CHUNK)

    def one_chunk(args):
        qc, start = args
        own = start // MOBA_BLOCK
        t_q = start + q_off
        k_own = lax.dynamic_index_in_dim(k_blk, own, axis=2, keepdims=False)
        v_own = lax.dynamic_index_in_dim(v_blk, own, axis=2, keepdims=False)
        s_own = jnp.einsum('bhqd,bhnd->bhqn', qc, k_own, preferred_element_type=jnp.float32) * scale
        causal = (own * MOBA_BLOCK + in_blk)[None, :] <= t_q[:, None]
        s_own = jnp.where(causal, s_own, NEG_INF)
        if n_sel == 0:
            p = jax.nn.softmax(s_own, axis=-1).astype(v.dtype)
            return jnp.einsum('bhqn,bhnd->bhqd', p, v_own)
        gate = jnp.einsum('bhqd,bhjd->bhqj', qc.astype(jnp.float32), k_mean)
        gate = jnp.where(blk_ids < own, gate, NEG_INF)
        _, sel = lax.top_k(gate, n_sel)
        sel_valid = sel < own
        k_sel = k_blk[b_idx, h_idx, sel]
        v_sel = v_blk[b_idx, h_idx, sel]
        s_sel = jnp.einsum('bhqd,bhqknd->bhqkn', qc, k_sel, preferred_element_type=jnp.float32) * scale
        s_sel = jnp.where(sel_valid[..., None], s_sel, NEG_INF)
        width = n_sel * MOBA_BLOCK
        scores = jnp.concatenate([s_sel.reshape(bsz, nh, MOBA_Q_CHUNK, width), s_own], axis=-1)
        p = jax.nn.softmax(scores, axis=-1).astype(v.dtype)
        p_sel = p[..., :width].reshape(bsz, nh, MOBA_Q_CHUNK, n_sel, MOBA_BLOCK)
        p_own = p[..., width:]
        return (jnp.einsum('bhqkn,bhqknd->bhqd', p_sel, v_sel)
                + jnp.einsum('bhqn,bhnd->bhqd', p_own, v_own))

    out = lax.map(one_chunk, (q_chunks, starts))
    return jnp.moveaxis(out, 0, 2).reshape(bsz, nh, seq, dh)


def stick_breaking_attention(q, k, v):
    bsz, nh, seq, dh = q.shape
    scale = dh ** -0.5
    outs = []
    for i in range(seq // SB_Q_BLOCK):
        lo, hi = i * SB_Q_BLOCK, (i + 1) * SB_Q_BLOCK
        z = jnp.einsum('bhqd,bhkd->bhqk', q[:, :, lo:hi], k[:, :, :hi],
                       preferred_element_type=jnp.float32) * scale
        past = jnp.arange(hi)[None, :] < jnp.arange(lo, hi)[:, None]
        log_keep = jnp.where(past, jax.nn.log_sigmoid(-z), 0.0)
        log_stick = lax.cumsum(log_keep, axis=3, reverse=True) - log_keep
        w = jnp.where(past, jnp.exp(jax.nn.log_sigmoid(z) + log_stick), 0.0)
        outs.append(jnp.einsum('bhqk,bhkd->bhqd', w.astype(v.dtype), v[:, :, :hi]))
    return jnp.concatenate(outs, axis=2)


def banded_causal_attention(q, k, v, steps):
    bsz, nh, n, dh = q.shape
    scale = dh ** -0.5
    blk = steps
    n_blk = -(-n // blk)
    pad = ((0, 0), (0, 0), (0, n_blk * blk - n), (0, 0))
    qb = jnp.pad(q, pad).reshape(bsz, nh, n_blk, blk, dh)
    kb = jnp.pad(k, pad).reshape(bsz, nh, n_blk, blk, dh)
    vb = jnp.pad(v, pad).reshape(bsz, nh, n_blk, blk, dh)

    def with_prev(xb):
        prev = jnp.concatenate([jnp.zeros_like(xb[:, :, :1]), xb[:, :, :-1]], axis=2)
        return jnp.concatenate([prev, xb], axis=3)

    kk, vv = with_prev(kb), with_prev(vb)
    s = jnp.einsum('bhnqd,bhnkd->bhnqk', qb, kk, preferred_element_type=jnp.float32) * scale
    qi = jnp.arange(blk)[:, None]
    kj = jnp.arange(2 * blk)[None, :]
    dist = qi + blk - kj
    key_pos = jnp.arange(n_blk)[:, None, None] * blk - blk + kj[None]
    mask = (dist >= 0) & (dist <= steps) & (key_pos >= 0)
    s = jnp.where(mask, s, NEG_INF)
    lse = jax.nn.logsumexp(s, axis=-1)
    p = jnp.exp(s - lse[..., None]).astype(v.dtype)
    out = jnp.einsum('bhnqk,bhnkd->bhnqd', p, vv).reshape(bsz, nh, n_blk * blk, dh)[:, :, :n]
    return out, lse.reshape(bsz, nh, n_blk * blk)[:, :, :n]


def dilated_attention(q, k, v, window, dilation):
    bsz, nh, seq, dh = q.shape
    sub = seq // dilation

    def split(x):
        return jnp.transpose(x.reshape(bsz, nh, sub, dilation, dh), (0, 1, 3, 2, 4)).reshape(bsz, nh * dilation, sub, dh)

    out, lse = banded_causal_attention(split(q), split(k), split(v), window // dilation)
    out = jnp.transpose(out.reshape(bsz, nh, dilation, sub, dh), (0, 1, 3, 2, 4)).reshape(bsz, nh, seq, dh)
    lse = jnp.transpose(lse.reshape(bsz, nh, dilation, sub), (0, 1, 3, 2)).reshape(bsz, nh, seq)
    return out, lse


def dilated_mixture(q, k, v):
    outs, lses = [], []
    for g, (window, dilation) in enumerate(DILATED_PATTERNS):
        sl = slice(g * C_HEADS_PER_PATTERN, (g + 1) * C_HEADS_PER_PATTERN)
        o, l = dilated_attention(q[:, sl], k[:, sl], v[:, sl], window, dilation)
        outs.append(o)
        lses.append(l)
    alpha = jax.nn.softmax(jnp.stack(lses, axis=0), axis=0)
    mixed = jnp.sum(alpha[..., None] * jnp.stack(outs, axis=0).astype(jnp.float32), axis=0)
    return mixed.astype(v.dtype)


def cross_attention(h, mem_n, w_q, w_kv, w_o):
    bsz, seq, _ = h.shape
    m = mem_n.shape[1]
    q = jnp.einsum('bsd,de->bse', h, w_q).reshape(bsz, seq, CROSS_HEADS, CROSS_HEAD_DIM)
    kv = jnp.einsum('bmd,de->bme', mem_n, w_kv).reshape(bsz, m, 2, CROSS_HEADS, CROSS_HEAD_DIM)
    s = jnp.einsum('bshd,bmhd->bhsm', q, kv[:, :, 0], preferred_element_type=jnp.float32) * CROSS_HEAD_DIM ** -0.5
    p = jax.nn.softmax(s, axis=-1).astype(h.dtype)
    o = jnp.einsum('bhsm,bmhd->bshd', p, kv[:, :, 1]).reshape(bsz, seq, CROSS_HEADS * CROSS_HEAD_DIM)
    return jnp.einsum('bse,ed->bsd', o, w_o)


def setup_inputs(seed: int = 0) -> dict:
    key = jax.random.key(seed)
    ks = jax.random.split(key, 17)
    f32 = jnp.float32

    def nrm(k, shape, scale):
        return jax.random.normal(k, shape, f32) * scale

    def gain(k, shape):
        return 1.0 + 0.02 * jax.random.normal(k, shape, f32)

    cw = CROSS_HEADS * CROSS_HEAD_DIM
    return {
        "x": nrm(ks[0], (BATCH, SEQ, D_MODEL), 1.0),
        "mem": nrm(ks[1], (BATCH, MEM_LEN, D_MODEL), 1.0),
        "g_mix": gain(ks[2], (DEPTH, D_MODEL)),
        "w_in": nrm(ks[3], (DEPTH, D_MODEL, 3 * N_MIX_HEADS * HEAD_DIM), D_MODEL ** -0.5),
        "g_out_a": gain(ks[4], (DEPTH, N_HEADS_A * HEAD_DIM)),
        "g_out_b": gain(ks[5], (DEPTH, N_HEADS_B * HEAD_DIM)),
        "g_out_c": gain(ks[6], (DEPTH, C_HEADS_PER_PATTERN * HEAD_DIM)),
        "w_out": nrm(ks[7], (DEPTH, OUT_WIDTH, D_MODEL), OUT_WIDTH ** -0.5),
        "g_cross": gain(ks[8], (DEPTH, D_MODEL)),
        "g_mem": gain(ks[9], (D_MODEL,)),
        "w_cq": nrm(ks[10], (DEPTH, D_MODEL, cw), D_MODEL ** -0.5),
        "w_ckv": nrm(ks[11], (DEPTH, D_MODEL, 2 * cw), D_MODEL ** -0.5),
        "w_co": nrm(ks[12], (DEPTH, cw, D_MODEL), cw ** -0.5),
        "g_mlp": gain(ks[13], (DEPTH, D_MODEL)),
        "w_up": nrm(ks[14], (DEPTH, D_MODEL, D_FF), D_MODEL ** -0.5),
        "w_down": nrm(ks[15], (DEPTH, D_FF, D_MODEL), D_FF ** -0.5),
        "g_final": gain(ks[16], (D_MODEL,)),
    }


def reference(x, mem, g_mix, w_in, g_out_a, g_out_b, g_out_c, w_out, g_cross, g_mem,
              w_cq, w_ckv, w_co, g_mlp, w_up, w_down, g_final):
    bsz, seq, _ = x.shape
    cos, sin = rope_tables(seq)
    mem_n = rms_norm(mem, g_mem)
    ha, hb = N_HEADS_A, N_HEADS_A + N_HEADS_B
    for l in range(DEPTH):
        h = rms_norm(x, g_mix[l])
        qkv = jnp.einsum('bsd,de->bse', h, w_in[l]).reshape(bsz, seq, 3, N_MIX_HEADS, HEAD_DIM)
        q = jnp.transpose(qkv[:, :, 0], (0, 2, 1, 3))
        k = jnp.transpose(qkv[:, :, 1], (0, 2, 1, 3))
        v = jnp.transpose(qkv[:, :, 2], (0, 2, 1, 3))
        o_a = moba_attention(apply_rope(q[:, :ha], cos, sin), apply_rope(k[:, :ha], cos, sin), v[:, :ha])
        o_b = stick_breaking_attention(q[:, ha:hb], k[:, ha:hb], v[:, ha:hb])
        o_c = dilated_mixture(apply_rope(q[:, hb:], cos, sin), apply_rope(k[:, hb:], cos, sin), v[:, hb:])
        merged = jnp.concatenate([
            rms_norm(heads_to_channels(o_a), g_out_a[l]),
            rms_norm(heads_to_channels(o_b), g_out_b[l]),
            rms_norm(heads_to_channels(o_c), g_out_c[l]),
        ], axis=-1)
        x = x + jnp.einsum('bse,ed->bsd', merged, w_out[l])
        x = x + cross_attention(rms_norm(x, g_cross[l]), mem_n, w_cq[l], w_ckv[l], w_co[l])
        u = jnp.einsum('bsd,df->bsf', rms_norm(x, g_mlp[l]), w_up[l])
        x = x + jnp.einsum('bsf,fd->bsd', jnp.square(jax.nn.relu(u)), w_down[l])
    return rms_norm(x, g_final)
```

```python
import functools

import jax
import jax.numpy as jnp
from jax import lax
from jax.experimental import pallas as pl
from jax.experimental.pallas import tpu as pltpu

HEAD_DIM = 128
N_MIX_HEADS = 16
N_HEADS_A = 4
N_HEADS_B = 6
N_HEADS_C = 6
DILATIONS = (1, 4, 16)
BAND = 128
C_HEADS_PER_PATTERN = N_HEADS_C // len(DILATIONS)
MOBA_BLOCK = 256
MOBA_TOPK = 3
SB_BLOCK = 256
ROPE_THETA = 10000.0
CROSS_HEADS = 4
CROSS_HEAD_DIM = 128
RMS_EPS = 1e-6
NEG_INF = -1e30
REMOVED = -3e38

LANES = 128
V7X_VMEM_LIMIT_BYTES = 56 * 1024 * 1024

_BF16 = jnp.bfloat16
_F32 = jnp.float32

SLAB_QA = 0
SLAB_KA = SLAB_QA + N_HEADS_A
SLAB_QC = SLAB_KA + N_HEADS_A
SLAB_KC = SLAB_QC + N_HEADS_C
N_ROPE_SLABS = SLAB_KC + N_HEADS_C
SLAB_QB = N_ROPE_SLABS
SLAB_KB = SLAB_QB + N_HEADS_B
SLAB_VA = SLAB_KB + N_HEADS_B
SLAB_VB = SLAB_VA + N_HEADS_A
SLAB_VC = SLAB_VB + N_HEADS_B
N_SLABS = SLAB_VC + N_HEADS_C


def _dot(a, b):
    return jnp.dot(a, b, preferred_element_type=_F32)


def _dot_nt(a, b):
    return lax.dot_general(a, b, (((1,), (1,)), ((), ())), preferred_element_type=_F32)


def _rms(x, g):
    ms = jnp.mean(x * x, axis=-1, keepdims=True)
    return x * lax.rsqrt(ms + RMS_EPS) * g


def _params(*semantics):
    return pltpu.CompilerParams(dimension_semantics=semantics, vmem_limit_bytes=V7X_VMEM_LIMIT_BYTES)


def _norm_matmul_kernel(*refs, segments, n_rope_tiles, has_res, tn):
    it = iter(refs)
    x_ref, g_ref, w_ref = next(it), next(it), next(it)
    cos_ref = sin_ref = res_ref = None
    if n_rope_tiles:
        cos_ref, sin_ref = next(it), next(it)
    if has_res:
        res_ref = next(it)
    o_ref, hn_ref = next(it), next(it)
    j = pl.program_id(1)

    @pl.when(j == 0)
    def _normalise():
        for a, b in segments:
            hn_ref[:, a:b] = _rms(x_ref[:, a:b], g_ref[:, a:b]).astype(_BF16)

    acc = _dot(hn_ref[...], w_ref[...])
    if has_res:
        acc = acc + res_ref[...]
    if not n_rope_tiles:
        o_ref[...] = acc.astype(o_ref.dtype)
        return

    @pl.when(j < n_rope_tiles)
    def _rotary():
        c, s = cos_ref[...], sin_ref[...]
        for h in range(tn // LANES):
            sl = acc[:, h * LANES:(h + 1) * LANES]
            o_ref[:, h * LANES:(h + 1) * LANES] = (sl * c + pltpu.roll(sl, LANES // 2, 1) * s).astype(o_ref.dtype)

    @pl.when(j >= n_rope_tiles)
    def _plain():
        o_ref[...] = acc.astype(o_ref.dtype)


def _norm_matmul(x, g, w, *, segments, tm, tn, out_dtype, rope=None, residual=None):
    m, k = x.shape
    n = w.shape[1]
    assert m % tm == 0 and n % tn == 0 and tn % LANES == 0
    n_rope_tiles = 0
    operands = [x, g.reshape(1, k), w]
    in_specs = [
        pl.BlockSpec((tm, k), lambda i, j: (i, 0)),
        pl.BlockSpec((1, k), lambda i, j: (0, 0)),
        pl.BlockSpec((k, tn), lambda i, j: (0, j)),
    ]
    if rope is not None:
        cos2, sin2, n_rope_cols, seq = rope
        assert n_rope_cols % tn == 0 and seq % tm == 0
        n_rope_tiles = n_rope_cols // tn
        tiles_per_seq = seq // tm
        operands += [cos2, sin2]
        in_specs += [pl.BlockSpec((tm, LANES), lambda i, j: (i % tiles_per_seq, 0))] * 2
    if residual is not None:
        operands.append(residual)
        in_specs.append(pl.BlockSpec((tm, tn), lambda i, j: (i, j)))
    return pl.pallas_call(
        functools.partial(_norm_matmul_kernel, segments=segments, n_rope_tiles=n_rope_tiles,
                          has_res=residual is not None, tn=tn),
        grid=(m // tm, n // tn),
        in_specs=in_specs,
        out_specs=pl.BlockSpec((tm, tn), lambda i, j: (i, j)),
        out_shape=jax.ShapeDtypeStruct((m, n), out_dtype),
        scratch_shapes=[pltpu.VMEM((tm, k), _BF16)],
        compiler_params=_params("parallel", "arbitrary"),
    )(*operands)


def _mlp_kernel(x_ref, g_ref, wu_ref, wd_ref, o_ref, hn_ref):
    @pl.when(pl.program_id(1) == 0)
    def _start():
        x = x_ref[...]
        hn_ref[...] = _rms(x, g_ref[...]).astype(_BF16)
        o_ref[...] = x

    u = _dot(hn_ref[...], wu_ref[...])
    r = jnp.square(jnp.maximum(u, 0.0)).astype(_BF16)
    o_ref[...] += _dot(r, wd_ref[...])


def _mlp(x, g, w_up, w_down, *, tm, tf):
    m, d = x.shape
    f = w_up.shape[1]
    assert m % tm == 0 and f % tf == 0
    return pl.pallas_call(
        _mlp_kernel,
        grid=(m // tm, f // tf),
        in_specs=[
            pl.BlockSpec((tm, d), lambda i, j: (i, 0)),
            pl.BlockSpec((1, d), lambda i, j: (0, 0)),
            pl.BlockSpec((d, tf), lambda i, j: (0, j)),
            pl.BlockSpec((tf, d), lambda i, j: (j, 0)),
        ],
        out_specs=pl.BlockSpec((tm, d), lambda i, j: (i, 0)),
        out_shape=jax.ShapeDtypeStruct((m, d), _F32),
        scratch_shapes=[pltpu.VMEM((tm, d), _BF16)],
        compiler_params=_params("parallel", "arbitrary"),
    )(x, g.reshape(1, d), w_up, w_down)


def _cross_kernel(x_ref, g_ref, wq_ref, kv_ref, wo_ref, o_ref):
    x = x_ref[...]
    hn = _rms(x, g_ref[...]).astype(_BF16)
    q = _dot(hn, wq_ref[...]).astype(_BF16)
    kv = kv_ref[0]
    width = CROSS_HEADS * CROSS_HEAD_DIM
    outs = []
    for h in range(CROSS_HEADS):
        sl = slice(h * CROSS_HEAD_DIM, (h + 1) * CROSS_HEAD_DIM)
        s = _dot_nt(q[:, sl], kv[:, sl]) * (CROSS_HEAD_DIM ** -0.5)
        e = jnp.exp(s - jnp.max(s, axis=-1, keepdims=True))
        p = e / jnp.sum(e, axis=-1, keepdims=True)
        outs.append(_dot(p.astype(_BF16), kv[:, width + h * CROSS_HEAD_DIM: width + (h + 1) * CROSS_HEAD_DIM]))
    o = jnp.concatenate(outs, axis=1).astype(_BF16)
    o_ref[...] = x + _dot(o, wo_ref[...])


def _cross(x, g, w_q, kv, w_o, *, seq, tm):
    m, d = x.shape
    width = w_q.shape[1]
    mem_len = kv.shape[1]
    assert seq % tm == 0
    tiles_per_seq = seq // tm
    return pl.pallas_call(
        _cross_kernel,
        grid=(m // tm,),
        in_specs=[
            pl.BlockSpec((tm, d), lambda i: (i, 0)),
            pl.BlockSpec((1, d), lambda i: (0, 0)),
            pl.BlockSpec((d, width), lambda i: (0, 0)),
            pl.BlockSpec((1, mem_len, 2 * width), lambda i: (i // tiles_per_seq, 0, 0)),
            pl.BlockSpec((width, d), lambda i: (0, 0)),
        ],
        out_specs=pl.BlockSpec((tm, d), lambda i: (i, 0)),
        out_shape=jax.ShapeDtypeStruct((m, d), _F32),
        compiler_params=_params("parallel"),
    )(x, g.reshape(1, d), w_q, kv, w_o)


def _moba_kernel(q_ref, k_ref, v_ref, o_ref, km_ref, *, n_blk):
    qi = pl.program_id(2)
    blk = MOBA_BLOCK
    scale = HEAD_DIM ** -0.5

    @pl.when(qi == 0)
    def _block_means():
        km_ref[...] = jnp.zeros_like(km_ref)
        for j in range(n_blk):
            kb = k_ref[0, j * blk:(j + 1) * blk, :].astype(_F32)
            km_ref[j:j + 1, :] = jnp.sum(kb, axis=0, keepdims=True) * (1.0 / blk)

    q = q_ref[0]
    km = km_ref[...]
    km_hi = km.astype(_BF16)
    km_lo = (km - km_hi.astype(_F32)).astype(_BF16)
    gate = _dot_nt(q, km_hi) + _dot_nt(q, km_lo)
    colf = lax.broadcasted_iota(jnp.int32, gate.shape, 1).astype(_F32)
    own = qi.astype(_F32)
    g = jnp.where(colf < own, gate, NEG_INF)
    picks = []
    for _ in range(MOBA_TOPK):
        mx = jnp.max(g, axis=-1, keepdims=True)
        idx = jnp.min(jnp.where(g == mx, colf, float(LANES)), axis=-1, keepdims=True)
        picks.append(idx)
        g = jnp.where(colf == idx, REMOVED, g)

    row = lax.broadcasted_iota(jnp.int32, (blk, blk), 0)
    col = lax.broadcasted_iota(jnp.int32, (blk, blk), 1)
    start = pl.multiple_of(qi * blk, blk)
    s = _dot_nt(q, k_ref[0, pl.ds(start, blk), :]) * scale
    s = jnp.where(col <= row, s, NEG_INF)
    m0 = jnp.max(s, axis=-1, keepdims=True)
    p = jnp.exp(s - m0)
    l0 = jnp.sum(p, axis=-1, keepdims=True)
    acc0 = _dot(p.astype(_BF16), v_ref[0, pl.ds(start, blk), :])

    def past_block(j, carry):
        m, l, acc = carry
        jf = j.astype(_F32)
        chosen = (picks[0] == jf) | (picks[1] == jf) | (picks[2] == jf)
        bias = jnp.where(chosen, 0.0, NEG_INF)
        st = pl.multiple_of(j * blk, blk)
        sj = _dot_nt(q, k_ref[0, pl.ds(st, blk), :]) * scale + bias
        m_new = jnp.maximum(m, jnp.max(sj, axis=-1, keepdims=True))
        a = jnp.exp(m - m_new)
        pj = jnp.exp(sj - m_new)
        l = a * l + jnp.sum(pj, axis=-1, keepdims=True)
        acc = a * acc + _dot(pj.astype(_BF16), v_ref[0, pl.ds(st, blk), :])
        return m_new, l, acc

    _, l, acc = lax.fori_loop(0, qi, past_block, (m0, l0, acc0))
    o_ref[0] = acc / l


def _moba(qkv, *, bsz, seq):
    n_blk = seq // MOBA_BLOCK
    assert seq % MOBA_BLOCK == 0 and MOBA_TOPK < n_blk <= LANES
    return pl.pallas_call(
        functools.partial(_moba_kernel, n_blk=n_blk),
        grid=(bsz, N_HEADS_A, n_blk),
        in_specs=[
            pl.BlockSpec((1, MOBA_BLOCK, LANES), lambda b, h, i: (b, i, SLAB_QA + h)),
            pl.BlockSpec((1, seq, LANES), lambda b, h, i: (b, 0, SLAB_KA + h)),
            pl.BlockSpec((1, seq, LANES), lambda b, h, i: (b, 0, SLAB_VA + h)),
        ],
        out_specs=pl.BlockSpec((1, MOBA_BLOCK, LANES), lambda b, h, i: (b, i, h)),
        out_shape=jax.ShapeDtypeStruct((bsz, seq, N_HEADS_A * HEAD_DIM), _F32),
        scratch_shapes=[pltpu.VMEM((LANES, LANES), _F32)],
        compiler_params=_params("parallel", "parallel", "arbitrary"),
    )(qkv, qkv, qkv)


def _sb_kernel(q_ref, k_ref, v_ref, o_ref):
    qi = pl.program_id(2)
    blk = SB_BLOCK
    scale = HEAD_DIM ** -0.5
    q = q_ref[0]
    row = lax.broadcasted_iota(jnp.int32, (blk, blk), 0)
    col = lax.broadcasted_iota(jnp.int32, (blk, blk), 1)
    later = (row > col).astype(_BF16)
    past = col < row

    def block(start, carry, diagonal):
        z = _dot_nt(q, k_ref[0, pl.ds(start, blk), :]) * scale
        sp = jnp.log1p(jnp.exp(-jnp.abs(z)))
        log_beta = jnp.minimum(z, 0.0) - sp
        log_keep = -jnp.maximum(z, 0.0) - sp
        if diagonal:
            log_keep = jnp.where(past, log_keep, 0.0)
        hi = log_keep.astype(_BF16)
        r1 = log_keep - hi.astype(_F32)
        mid = r1.astype(_BF16)
        lo = (r1 - mid.astype(_F32)).astype(_BF16)
        log_stick = _dot(hi, later) + _dot(mid, later) + _dot(lo, later)
        if carry is not None:
            log_stick = log_stick + carry
        w = jnp.exp(log_beta + log_stick)
        if diagonal:
            w = jnp.where(past, w, 0.0)
        out = _dot(w.astype(_BF16), v_ref[0, pl.ds(start, blk), :])
        return out, jnp.sum(log_keep, axis=-1, keepdims=True)

    acc0, c0 = block(pl.multiple_of(qi * blk, blk), None, True)

    def past_block(it, carry):
        acc, c = carry
        out, dc = block(pl.multiple_of((qi - 1 - it) * blk, blk), c, False)
        return acc + out, c + dc

    acc, _ = lax.fori_loop(0, qi, past_block, (acc0, c0))
    o_ref[0] = acc


def _stick_breaking(qkv, *, bsz, seq):
    assert seq % SB_BLOCK == 0
    return pl.pallas_call(
        _sb_kernel,
        grid=(bsz, N_HEADS_B, seq // SB_BLOCK),
        in_specs=[
            pl.BlockSpec((1, SB_BLOCK, LANES), lambda b, h, i: (b, i, SLAB_QB + h)),
            pl.BlockSpec((1, seq, LANES), lambda b, h, i: (b, 0, SLAB_KB + h)),
            pl.BlockSpec((1, seq, LANES), lambda b, h, i: (b, 0, SLAB_VB + h)),
        ],
        out_specs=pl.BlockSpec((1, SB_BLOCK, LANES), lambda b, h, i: (b, i, h)),
        out_shape=jax.ShapeDtypeStruct((bsz, seq, N_HEADS_B * HEAD_DIM), _F32),
        compiler_params=_params("parallel", "parallel", "parallel"),
    )(qkv, qkv, qkv)


def _dilated_kernel(q_ref, k_ref, v_ref, o_ref, qf, kf, vf, og, lg, *, seq):
    g = pl.program_id(2)
    scale = HEAD_DIM ** -0.5
    qf[...] = q_ref[0].astype(_F32)
    kf[...] = k_ref[0].astype(_F32)
    vf[...] = v_ref[0].astype(_F32)
    row1 = lax.broadcasted_iota(jnp.int32, (BAND, BAND), 0)
    col1 = lax.broadcasted_iota(jnp.int32, (BAND, BAND), 1)
    row2 = lax.broadcasted_iota(jnp.int32, (BAND, 2 * BAND), 0)
    col2 = lax.broadcasted_iota(jnp.int32, (BAND, 2 * BAND), 1)
    own_only = col1 <= row1
    prev_and_own = ((col2 < BAND) & (col2 >= row2)) | ((col2 >= BAND) & (col2 - BAND <= row2))

    def rows(ref, start, size, d):
        return ref[pl.ds(start, size, stride=d), :] if d > 1 else ref[pl.ds(start, size), :]

    def unit(gi, d, q_start, k_start, n_keys, mask):
        qu = rows(qf, q_start, BAND, d).astype(_BF16)
        ku = rows(kf, k_start, n_keys, d).astype(_BF16)
        vu = rows(vf, k_start, n_keys, d).astype(_BF16)
        s = jnp.where(mask, _dot_nt(qu, ku) * scale, NEG_INF)
        mx = jnp.max(s, axis=-1, keepdims=True)
        e = jnp.exp(s - mx)
        den = jnp.sum(e, axis=-1, keepdims=True)
        out = _dot((e / den).astype(_BF16), vu)
        lse = mx + jnp.log(den)
        if d > 1:
            og[gi, pl.ds(q_start, BAND, stride=d), :] = out
            lg[gi, pl.ds(q_start, BAND, stride=d), :] = jnp.broadcast_to(lse, (BAND, LANES))
        else:
            og[gi, pl.ds(q_start, BAND), :] = out
            lg[gi, pl.ds(q_start, BAND), :] = jnp.broadcast_to(lse, (BAND, LANES))

    def pattern(gi, d):
        n_blk = seq // (BAND * d)

        def residue(r, _):
            unit(gi, d, r, r, BAND, own_only)

            def later_block(nb, _):
                unit(gi, d, r + nb * (BAND * d), r + (nb - 1) * (BAND * d), 2 * BAND, prev_and_own)
                return 0

            return lax.fori_loop(1, n_blk, later_block, 0)

        lax.fori_loop(0, d, residue, 0)

    for gi, d in enumerate(DILATIONS):
        pl.when(g == gi)(functools.partial(pattern, gi, d))

    @pl.when(g == len(DILATIONS) - 1)
    def _mix():
        chunk = 256

        def mix_chunk(c, _):
            sl = pl.ds(pl.multiple_of(c * chunk, chunk), chunk)
            l = [lg[gi, sl, :] for gi in range(len(DILATIONS))]
            mx = jnp.maximum(jnp.maximum(l[0], l[1]), l[2])
            e = [jnp.exp(li - mx) for li in l]
            den = e[0] + e[1] + e[2]
            o_ref[0, sl, :] = (e[0] / den) * og[0, sl, :] + (e[1] / den) * og[1, sl, :] + (e[2] / den) * og[2, sl, :]
            return 0

        lax.fori_loop(0, seq // chunk, mix_chunk, 0)


def _dilated(qkv, *, bsz, seq):
    assert seq % (BAND * max(DILATIONS)) == 0 and seq % 256 == 0
    n_pat = len(DILATIONS)
    per = C_HEADS_PER_PATTERN
    return pl.pallas_call(
        functools.partial(_dilated_kernel, seq=seq),
        grid=(bsz, per, n_pat),
        in_specs=[
            pl.BlockSpec((1, seq, LANES), lambda b, j, g: (b, 0, SLAB_QC + g * per + j)),
            pl.BlockSpec((1, seq, LANES), lambda b, j, g: (b, 0, SLAB_KC + g * per + j)),
            pl.BlockSpec((1, seq, LANES), lambda b, j, g: (b, 0, SLAB_VC + g * per + j)),
        ],
        out_specs=pl.BlockSpec((1, seq, LANES), lambda b, j, g: (b, 0, j)),
        out_shape=jax.ShapeDtypeStruct((bsz, seq, per * HEAD_DIM), _F32),
        scratch_shapes=[pltpu.VMEM((seq, LANES), _F32)] * 3 + [pltpu.VMEM((n_pat, seq, LANES), _F32)] * 2,
        compiler_params=_params("parallel", "parallel", "arbitrary"),
    )(qkv, qkv, qkv)


def _final_norm_kernel(x_ref, g_ref, o_ref):
    o_ref[...] = _rms(x_ref[...], g_ref[...])


def _final_norm(x, g, *, tm):
    m, d = x.shape
    return pl.pallas_call(
        _final_norm_kernel,
        grid=(m // tm,),
        in_specs=[pl.BlockSpec((tm, d), lambda i: (i, 0)), pl.BlockSpec((1, d), lambda i: (0, 0))],
        out_specs=pl.BlockSpec((tm, d), lambda i: (i, 0)),
        out_shape=jax.ShapeDtypeStruct((m, d), _F32),
        compiler_params=_params("parallel"),
    )(x, g.reshape(1, d))


def _rope_tables(seq):
    inv_freq = 1.0 / (ROPE_THETA ** (jnp.arange(0, HEAD_DIM, 2, dtype=_F32) / HEAD_DIM))
    ang = jnp.arange(seq, dtype=_F32)[:, None] * inv_freq[None, :]
    cos, sin = jnp.cos(ang), jnp.sin(ang)
    return jnp.concatenate([cos, cos], axis=-1), jnp.concatenate([-sin, sin], axis=-1)


def _permute_w_in(w_in):
    depth, d, _ = w_in.shape
    w = w_in.reshape(depth, d, 3, N_MIX_HEADS, HEAD_DIM)
    ha, hb = N_HEADS_A, N_HEADS_A + N_HEADS_B
    parts = [w[:, :, 0, :ha], w[:, :, 1, :ha], w[:, :, 0, hb:], w[:, :, 1, hb:],
             w[:, :, 0, ha:hb], w[:, :, 1, ha:hb], w[:, :, 2]]
    return jnp.concatenate(parts, axis=2).reshape(depth, d, N_SLABS * HEAD_DIM).astype(_BF16)


def kernel(x, mem, g_mix, w_in, g_out_a, g_out_b, g_out_c, w_out, g_cross, g_mem,
           w_cq, w_ckv, w_co, g_mlp, w_up, w_down, g_final):
    bsz, seq, d = x.shape
    depth = w_in.shape[0]
    mem_len = mem.shape[1]
    m = bsz * seq
    tm = min(512, seq)
    cos2, sin2 = _rope_tables(seq)
    w_in_p = _permute_w_in(w_in)
    w_out_b, w_cq_b, w_ckv_b, w_co_b = (w.astype(_BF16) for w in (w_out, w_cq, w_ckv, w_co))
    w_up_b, w_down_b = w_up.astype(_BF16), w_down.astype(_BF16)
    g_out = jnp.concatenate([g_out_a, g_out_b, g_out_c], axis=-1)
    wa, wb = N_HEADS_A * HEAD_DIM, (N_HEADS_A + N_HEADS_B) * HEAD_DIM
    out_segments = ((0, wa), (wa, wb), (wb, wb + C_HEADS_PER_PATTERN * HEAD_DIM))

    xf = x.reshape(m, d)
    memf = mem.reshape(bsz * mem_len, d)
    for l in range(depth):
        qkv = _norm_matmul(xf, g_mix[l], w_in_p[l], segments=((0, d),), tm=tm, tn=512, out_dtype=_BF16,
                           rope=(cos2, sin2, N_ROPE_SLABS * HEAD_DIM, seq)).reshape(bsz, seq, -1)
        merged = jnp.concatenate([_moba(qkv, bsz=bsz, seq=seq), _stick_breaking(qkv, bsz=bsz, seq=seq),
                                  _dilated(qkv, bsz=bsz, seq=seq)], axis=-1).reshape(m, -1)
        xf = _norm_matmul(merged, g_out[l], w_out_b[l], segments=out_segments, tm=tm, tn=512,
                          out_dtype=_F32, residual=xf)
        kv = _norm_matmul(memf, g_mem, w_ckv_b[l], segments=((0, d),), tm=min(512, bsz * mem_len), tn=512,
                          out_dtype=_BF16).reshape(bsz, mem_len, -1)
        xf = _cross(xf, g_cross[l], w_cq_b[l], kv, w_co_b[l], seq=seq, tm=tm)
        xf = _mlp(xf, g_mlp[l], w_up_b[l], w_down_b[l], tm=tm, tf=512)
    return _final_norm(xf, g_final, tm=tm).reshape(bsz, seq, d)
```

```python
import functools

import jax
import jax.numpy as jnp
from jax import lax
from jax.experimental import pallas as pl
from jax.experimental.pallas import tpu as pltpu

HEAD_DIM = 128
N_MIX_HEADS = 16
N_HEADS_A = 4
N_HEADS_B = 6
N_HEADS_C = 6
DILATIONS = (1, 4, 16)
BAND = 128
C_HEADS_PER_PATTERN = N_HEADS_C // len(DILATIONS)
MOBA_BLOCK = 256
MOBA_TOPK = 3
SB_BLOCK = 256
BLOCKS_PER_STEP = 2
HEADS_PER_STEP = 2
DILATED_UNITS_PER_STEP = 4
ROPE_THETA = 10000.0
CROSS_HEADS = 4
CROSS_HEAD_DIM = 128
RMS_EPS = 1e-6
NEG_INF = -1e30
REMOVED = -3e38
QK_SCALE = HEAD_DIM ** -0.5

LANES = 128
V7X_VMEM_LIMIT_BYTES = 56 * 1024 * 1024

_BF16 = jnp.bfloat16
_F32 = jnp.float32

HEAD_A0, HEAD_B0, HEAD_C0 = 0, N_HEADS_A, N_HEADS_A + N_HEADS_B


def _dot(a, b):
    return jnp.dot(a, b, preferred_element_type=_F32)


def _dot_nt(a, b):
    return lax.dot_general(a, b, (((1,), (1,)), ((), ())), preferred_element_type=_F32)


def _rms(x, g):
    ms = jnp.mean(x * x, axis=-1, keepdims=True)
    return x * lax.rsqrt(ms + RMS_EPS) * g


def _params(*semantics):
    return pltpu.CompilerParams(dimension_semantics=semantics, vmem_limit_bytes=V7X_VMEM_LIMIT_BYTES)


def _norm_matmul_kernel(x_ref, g_ref, w_ref, o_ref, hn_ref):
    @pl.when(pl.program_id(1) == 0)
    def _normalise():
        hn_ref[...] = _rms(x_ref[...], g_ref[...]).astype(_BF16)

    o_ref[...] = _dot(hn_ref[...], w_ref[...]).astype(o_ref.dtype)


def _norm_matmul(x, g, w, *, tm, tn, out_dtype):
    m, k = x.shape
    n = w.shape[1]
    assert m % tm == 0 and n % tn == 0
    return pl.pallas_call(
        _norm_matmul_kernel,
        grid=(m // tm, n // tn),
        in_specs=[
            pl.BlockSpec((tm, k), lambda i, j: (i, 0)),
            pl.BlockSpec((1, k), lambda i, j: (0, 0)),
            pl.BlockSpec((k, tn), lambda i, j: (0, j)),
        ],
        out_specs=pl.BlockSpec((tm, tn), lambda i, j: (i, j)),
        out_shape=jax.ShapeDtypeStruct((m, n), out_dtype),
        scratch_shapes=[pltpu.VMEM((tm, k), _BF16)],
        compiler_params=_params("parallel", "arbitrary"),
    )(x, g.reshape(1, k), w)


def _in_proj_kernel(*refs, scale, rotary_heads):
    if any(rotary_heads):
        x_ref, g_ref, w_ref, cos_ref, sin_ref, o_ref = refs
        c, s = cos_ref[...], sin_ref[...]
    else:
        x_ref, g_ref, w_ref, o_ref = refs
    acc = _dot(_rms(x_ref[...], g_ref[...]).astype(_BF16), w_ref[...])
    for h, rotary in enumerate(rotary_heads):
        sl = slice(h * LANES, (h + 1) * LANES)
        a = acc[:, sl]
        if rotary:
            a = a * c + pltpu.roll(a, LANES // 2, 1) * s
        elif scale != 1.0:
            a = a * scale
        o_ref[:, sl] = a.astype(o_ref.dtype)


def _in_proj(x, g, w, *, tm, seq, scale=1.0, rope=None):
    m, k = x.shape
    n = w.shape[1]
    assert m % tm == 0 and seq % tm == 0 and n == N_MIX_HEADS * HEAD_DIM
    rotary_heads = tuple(rope is not None and not HEAD_B0 <= h < HEAD_C0 for h in range(N_MIX_HEADS))
    tiles_per_seq = seq // tm
    operands = [x, g.reshape(1, k), w]
    in_specs = [
        pl.BlockSpec((tm, k), lambda i: (i, 0)),
        pl.BlockSpec((1, k), lambda i: (0, 0)),
        pl.BlockSpec((k, n), lambda i: (0, 0)),
    ]
    if rope is not None:
        operands += list(rope)
        in_specs += [pl.BlockSpec((tm, LANES), lambda i: (i % tiles_per_seq, 0))] * 2
    return pl.pallas_call(
        functools.partial(_in_proj_kernel, scale=scale, rotary_heads=rotary_heads),
        grid=(m // tm,),
        in_specs=in_specs,
        out_specs=pl.BlockSpec((tm, n), lambda i: (i, 0)),
        out_shape=jax.ShapeDtypeStruct((m, n), _BF16),
        compiler_params=_params("parallel"),
    )(*operands)


def _out_proj_kernel(a_ref, b_ref, c_ref, g_ref, w_ref, x_ref, o_ref):
    acc = x_ref[...]
    off = 0
    for ref in (a_ref, b_ref, c_ref):
        width = ref.shape[1]
        hn = _rms(ref[...], g_ref[:, off:off + width]).astype(_BF16)
        acc = acc + _dot(hn, w_ref[off:off + width, :])
        off += width
    o_ref[...] = acc


def _out_proj(parts, g, w, x, *, tm):
    m, d = x.shape
    widths = [p.shape[1] for p in parts]
    assert sum(widths) == w.shape[0] and m % tm == 0
    return pl.pallas_call(
        _out_proj_kernel,
        grid=(m // tm,),
        in_specs=[pl.BlockSpec((tm, wd), lambda i: (i, 0)) for wd in widths] + [
            pl.BlockSpec((1, w.shape[0]), lambda i: (0, 0)),
            pl.BlockSpec(w.shape, lambda i: (0, 0)),
            pl.BlockSpec((tm, d), lambda i: (i, 0)),
        ],
        out_specs=pl.BlockSpec((tm, d), lambda i: (i, 0)),
        out_shape=jax.ShapeDtypeStruct((m, d), _F32),
        compiler_params=_params("parallel"),
    )(*parts, g.reshape(1, -1), w, x)


def _mlp_kernel(x_ref, g_ref, wu_ref, wd_ref, o_ref, hn_ref):
    @pl.when(pl.program_id(1) == 0)
    def _start():
        x = x_ref[...]
        hn_ref[...] = _rms(x, g_ref[...]).astype(_BF16)
        o_ref[...] = x

    u = _dot(hn_ref[...], wu_ref[...])
    r = jnp.square(jnp.maximum(u, 0.0)).astype(_BF16)
    o_ref[...] += _dot(r, wd_ref[...])


def _mlp(x, g, w_up, w_down, *, tm, tf):
    m, d = x.shape
    f = w_up.shape[1]
    assert m % tm == 0 and f % tf == 0
    return pl.pallas_call(
        _mlp_kernel,
        grid=(m // tm, f // tf),
        in_specs=[
            pl.BlockSpec((tm, d), lambda i, j: (i, 0)),
            pl.BlockSpec((1, d), lambda i, j: (0, 0)),
            pl.BlockSpec((d, tf), lambda i, j: (0, j)),
            pl.BlockSpec((tf, d), lambda i, j: (j, 0)),
        ],
        out_specs=pl.BlockSpec((tm, d), lambda i, j: (i, 0)),
        out_shape=jax.ShapeDtypeStruct((m, d), _F32),
        scratch_shapes=[pltpu.VMEM((tm, d), _BF16)],
        compiler_params=_params("parallel", "arbitrary"),
    )(x, g.reshape(1, d), w_up, w_down)


def _cross_kernel(x_ref, g_ref, wq_ref, kv_ref, wo_ref, o_ref):
    x = x_ref[...]
    hn = _rms(x, g_ref[...]).astype(_BF16)
    q = _dot(hn, wq_ref[...]).astype(_BF16)
    kv = kv_ref[0]
    width = CROSS_HEADS * CROSS_HEAD_DIM
    outs = []
    for h in range(CROSS_HEADS):
        sl = slice(h * CROSS_HEAD_DIM, (h + 1) * CROSS_HEAD_DIM)
        s = _dot_nt(q[:, sl], kv[:, sl]) * (CROSS_HEAD_DIM ** -0.5)
        e = jnp.exp(s - jnp.max(s, axis=-1, keepdims=True))
        p = e / jnp.sum(e, axis=-1, keepdims=True)
        outs.append(_dot(p.astype(_BF16), kv[:, width + h * CROSS_HEAD_DIM: width + (h + 1) * CROSS_HEAD_DIM]))
    o = jnp.concatenate(outs, axis=1).astype(_BF16)
    o_ref[...] = x + _dot(o, wo_ref[...])


def _cross(x, g, w_q, kv, w_o, *, seq, tm):
    m, d = x.shape
    width = w_q.shape[1]
    mem_len = kv.shape[1]
    assert seq % tm == 0
    tiles_per_seq = seq // tm
    return pl.pallas_call(
        _cross_kernel,
        grid=(m // tm,),
        in_specs=[
            pl.BlockSpec((tm, d), lambda i: (i, 0)),
            pl.BlockSpec((1, d), lambda i: (0, 0)),
            pl.BlockSpec((d, width), lambda i: (0, 0)),
            pl.BlockSpec((1, mem_len, 2 * width), lambda i: (i // tiles_per_seq, 0, 0)),
            pl.BlockSpec((width, d), lambda i: (0, 0)),
        ],
        out_specs=pl.BlockSpec((tm, d), lambda i: (i, 0)),
        out_shape=jax.ShapeDtypeStruct((m, d), _F32),
        compiler_params=_params("parallel"),
    )(x, g.reshape(1, d), w_q, kv, w_o)


def _moba_kernel(q_ref, k_ref, v_ref, o_ref, km_ref, onehot_ref, *, n_blk):
    qi = pl.program_id(2)
    blk = MOBA_BLOCK
    kb = BLOCKS_PER_STEP * blk
    heads = q_ref.shape[2] // LANES

    @pl.when(qi == 0)
    def _prepare():
        km_ref[...] = jnp.zeros_like(km_ref)
        for h in range(heads):
            for j in range(n_blk):
                rows = k_ref[0, j * blk:(j + 1) * blk, h * LANES:(h + 1) * LANES].astype(_F32)
                km_ref[h, j:j + 1, :] = jnp.sum(rows, axis=0, keepdims=True) * (1.0 / blk)
        key = lax.broadcasted_iota(jnp.int32, onehot_ref.shape, 0)
        lane = lax.broadcasted_iota(jnp.int32, onehot_ref.shape, 1)
        onehot_ref[...] = ((key >= lane * blk) & (key < (lane + 1) * blk)).astype(_BF16)

    colf = lax.broadcasted_iota(jnp.int32, (kb, LANES), 1).astype(_F32)
    rowl = lax.broadcasted_iota(jnp.int32, (kb, LANES), 0)
    row = lax.broadcasted_iota(jnp.int32, (kb, kb), 0)
    col = lax.broadcasted_iota(jnp.int32, (kb, kb), 1)
    own = (qi * BLOCKS_PER_STEP).astype(_F32)
    for p in range(1, BLOCKS_PER_STEP):
        own = own + (rowl >= p * blk).astype(_F32)

    def masked_queries(h):
        q = q_ref[0, :, h * LANES:(h + 1) * LANES]
        km = km_ref[h]
        km_hi = km.astype(_BF16)
        km_lo = (km - km_hi.astype(_F32)).astype(_BF16)
        gate = _dot_nt(q, km_hi) + _dot_nt(q, km_lo)
        g = jnp.where(colf < own, gate, NEG_INF)
        keep = colf == own
        for _ in range(MOBA_TOPK):
            mx = jnp.max(g, axis=-1, keepdims=True)
            idx = jnp.min(jnp.where(g == mx, colf, float(LANES)), axis=-1, keepdims=True)
            taken = colf == idx
            keep = keep | taken
            g = jnp.where(taken, REMOVED, g)
        return jnp.concatenate([q, jnp.where(keep, 0.0, NEG_INF).astype(_BF16)], axis=1)

    def key_blocks(h, sbi, q_aug, carry, diagonal):
        hs = slice(h * LANES, (h + 1) * LANES)
        start = pl.multiple_of(sbi * kb, kb)
        k_aug = jnp.concatenate([k_ref[0, pl.ds(start, kb), hs], onehot_ref[pl.ds(start, kb), :]], axis=1)
        s = _dot_nt(q_aug, k_aug)
        if diagonal:
            s = jnp.where(col <= row, s, NEG_INF)
        mx = jnp.max(s, axis=-1, keepdims=True)
        if carry is None:
            p = jnp.exp(s - mx)
            return mx, jnp.sum(p, axis=-1, keepdims=True), _dot(p.astype(_BF16), v_ref[0, pl.ds(start, kb), hs])
        m, l, acc = carry
        m_new = jnp.maximum(m, mx)
        a = jnp.exp(m - m_new)
        p = jnp.exp(s - m_new)
        l = a * l + jnp.sum(p, axis=-1, keepdims=True)
        acc = a * acc + _dot(p.astype(_BF16), v_ref[0, pl.ds(start, kb), hs])
        return m_new, l, acc

    q_aug = [masked_queries(h) for h in range(heads)]
    carry0 = tuple(key_blocks(h, qi, q_aug[h], None, True) for h in range(heads))

    def past_blocks(it, carry):
        return tuple(key_blocks(h, qi - 1 - it, q_aug[h], carry[h], False) for h in range(heads))

    carry = lax.fori_loop(0, qi, past_blocks, carry0)
    for h in range(heads):
        _, l, acc = carry[h]
        o_ref[0, :, h * LANES:(h + 1) * LANES] = acc / l


def _head_group_specs(seq, q_rows, head0, hp):
    assert head0 % hp == 0
    w = hp * LANES
    return [
        pl.BlockSpec((1, q_rows, w), lambda b, h, i: (b, i, head0 // hp + h)),
        pl.BlockSpec((1, seq, w), lambda b, h, i: (b, 0, head0 // hp + h)),
        pl.BlockSpec((1, seq, w), lambda b, h, i: (b, 0, head0 // hp + h)),
    ]


def _moba(q, k, v, *, bsz, seq):
    n_blk = seq // MOBA_BLOCK
    hp = HEADS_PER_STEP
    tq = MOBA_BLOCK * BLOCKS_PER_STEP
    assert seq % tq == 0 and MOBA_TOPK < n_blk <= LANES and N_HEADS_A % hp == 0
    return pl.pallas_call(
        functools.partial(_moba_kernel, n_blk=n_blk),
        grid=(bsz, N_HEADS_A // hp, seq // tq),
        in_specs=_head_group_specs(seq, tq, HEAD_A0, hp),
        out_specs=pl.BlockSpec((1, tq, hp * LANES), lambda b, h, i: (b, i, h)),
        out_shape=jax.ShapeDtypeStruct((bsz, seq, N_HEADS_A * HEAD_DIM), _F32),
        scratch_shapes=[pltpu.VMEM((hp, LANES, LANES), _F32), pltpu.VMEM((seq, LANES), _BF16)],
        compiler_params=_params("parallel", "parallel", "arbitrary"),
    )(q, k, v)


def _sb_kernel(q_ref, k_ref, v_ref, o_ref):
    qi = pl.program_id(2)
    blk = SB_BLOCK
    kb = BLOCKS_PER_STEP * blk
    heads = q_ref.shape[2] // LANES
    row2 = lax.broadcasted_iota(jnp.int32, (2 * blk, blk), 0)
    col2 = lax.broadcasted_iota(jnp.int32, (2 * blk, blk), 1)
    later = ((row2 > col2) & (row2 < blk) | (row2 - blk > col2)).astype(_BF16)
    past = lax.broadcasted_iota(jnp.int32, (kb, kb), 1) < lax.broadcasted_iota(jnp.int32, (kb, kb), 0)

    def suffix_sums(x):
        hi = x.astype(_BF16)
        lo = (x - hi.astype(_F32)).astype(_BF16)
        return _dot(jnp.concatenate([hi, lo], axis=1), later)

    def key_blocks(h, sbi, carry, diagonal):
        hs = slice(h * LANES, (h + 1) * LANES)
        start = pl.multiple_of(sbi * kb, kb)
        z = _dot_nt(q_ref[0, :, hs], k_ref[0, pl.ds(start, kb), hs])
        log_beta = jnp.minimum(z, 0.0) - jnp.log(1.0 + jnp.exp(-jnp.abs(z)))
        log_keep = log_beta - z
        if diagonal:
            log_keep = jnp.where(past, log_keep, 0.0)
        sticks = []
        tail = None if carry is None else carry[1]
        for p in reversed(range(BLOCKS_PER_STEP)):
            lk = log_keep[:, p * blk:(p + 1) * blk]
            st = suffix_sums(lk)
            sticks.append(st if tail is None else st + tail)
            total = jnp.sum(lk, axis=-1, keepdims=True)
            tail = total if tail is None else tail + total
        w = jnp.exp(log_beta + jnp.concatenate(sticks[::-1], axis=1))
        if diagonal:
            w = jnp.where(past, w, 0.0)
        out = _dot(w.astype(_BF16), v_ref[0, pl.ds(start, kb), hs])
        return (out if carry is None else carry[0] + out), tail

    carry0 = tuple(key_blocks(h, qi, None, True) for h in range(heads))

    def past_blocks(it, carry):
        return tuple(key_blocks(h, qi - 1 - it, carry[h], False) for h in range(heads))

    carry = lax.fori_loop(0, qi, past_blocks, carry0)
    for h in range(heads):
        o_ref[0, :, h * LANES:(h + 1) * LANES] = carry[h][0]


def _stick_breaking(q, k, v, *, bsz, seq):
    hp = HEADS_PER_STEP
    tq = SB_BLOCK * BLOCKS_PER_STEP
    assert seq % tq == 0 and N_HEADS_B % hp == 0
    return pl.pallas_call(
        _sb_kernel,
        grid=(bsz, N_HEADS_B // hp, seq // tq),
        in_specs=_head_group_specs(seq, tq, HEAD_B0, hp),
        out_specs=pl.BlockSpec((1, tq, hp * LANES), lambda b, h, i: (b, i, h)),
        out_shape=jax.ShapeDtypeStruct((bsz, seq, N_HEADS_B * HEAD_DIM), _F32),
        compiler_params=_params("parallel", "parallel", "parallel"),
    )(q, k, v)


def _dilated_kernel(q_ref, k_ref, v_ref, o_ref, qf, kf, vf, og, lg, *, seq):
    g = pl.program_id(2)
    qf[...] = q_ref[0].astype(_F32)
    kf[...] = k_ref[0].astype(_F32)
    vf[...] = v_ref[0].astype(_F32)
    row2 = lax.broadcasted_iota(jnp.int32, (BAND, 2 * BAND), 0)
    col2 = lax.broadcasted_iota(jnp.int32, (BAND, 2 * BAND), 1)
    own_block = (col2 >= BAND) & (col2 - BAND <= row2)
    prev_block = (col2 < BAND) & (col2 >= row2)

    def rows(ref, start, d):
        return ref[pl.ds(start, BAND, stride=d), :] if d > 1 else ref[pl.ds(start, BAND), :]

    def unit(gi, d, r, nb):
        q_start = r + nb * (BAND * d)
        p_start = q_start - jnp.where(nb > 0, BAND * d, 0)
        qu = rows(qf, q_start, d).astype(_BF16)
        ku = jnp.concatenate([rows(kf, p_start, d), rows(kf, q_start, d)], axis=0).astype(_BF16)
        vu = jnp.concatenate([rows(vf, p_start, d), rows(vf, q_start, d)], axis=0).astype(_BF16)
        s = jnp.where(own_block | (prev_block & (nb > 0)), _dot_nt(qu, ku), NEG_INF)
        mx = jnp.max(s, axis=-1, keepdims=True)
        e = jnp.exp(s - mx)
        den = jnp.sum(e, axis=-1, keepdims=True)
        out = _dot((e / den).astype(_BF16), vu)
        lse = mx + jnp.log(den)
        dst = pl.ds(q_start, BAND, stride=d) if d > 1 else pl.ds(q_start, BAND)
        og[gi, dst, :] = out
        lg[gi, dst, :] = jnp.broadcast_to(lse, (BAND, LANES))

    def pattern(gi, d):
        def unit_group(i, _):
            for u in range(DILATED_UNITS_PER_STEP):
                idx = i * DILATED_UNITS_PER_STEP + u
                if d == 1:
                    unit(gi, d, 0, idx)
                else:
                    unit(gi, d, lax.rem(idx, d), lax.div(idx, d))
            return 0

        lax.fori_loop(0, seq // (BAND * DILATED_UNITS_PER_STEP), unit_group, 0)

    for gi, d in enumerate(DILATIONS):
        pl.when(g == gi)(functools.partial(pattern, gi, d))

    @pl.when(g == len(DILATIONS) - 1)
    def _mix():
        chunk = 256

        def mix_chunk(c, _):
            sl = pl.ds(pl.multiple_of(c * chunk, chunk), chunk)
            l = [lg[gi, sl, :] for gi in range(len(DILATIONS))]
            mx = jnp.maximum(jnp.maximum(l[0], l[1]), l[2])
            e = [jnp.exp(li - mx) for li in l]
            den = e[0] + e[1] + e[2]
            o_ref[0, sl, :] = (e[0] / den) * og[0, sl, :] + (e[1] / den) * og[1, sl, :] + (e[2] / den) * og[2, sl, :]
            return 0

        lax.fori_loop(0, seq // chunk, mix_chunk, 0)


def _dilated(q, k, v, *, bsz, seq):
    assert seq % (BAND * max(DILATIONS)) == 0 and seq % 256 == 0
    n_pat = len(DILATIONS)
    per = C_HEADS_PER_PATTERN
    spec = pl.BlockSpec((1, seq, LANES), lambda b, j, g: (b, 0, HEAD_C0 + g * per + j))
    return pl.pallas_call(
        functools.partial(_dilated_kernel, seq=seq),
        grid=(bsz, per, n_pat),
        in_specs=[spec, spec, spec],
        out_specs=pl.BlockSpec((1, seq, LANES), lambda b, j, g: (b, 0, j)),
        out_shape=jax.ShapeDtypeStruct((bsz, seq, per * HEAD_DIM), _F32),
        scratch_shapes=[pltpu.VMEM((seq, LANES), _F32)] * 3 + [pltpu.VMEM((n_pat, seq, LANES), _F32)] * 2,
        compiler_params=_params("parallel", "parallel", "arbitrary"),
    )(q, k, v)


def _final_norm_kernel(x_ref, g_ref, o_ref):
    o_ref[...] = _rms(x_ref[...], g_ref[...])


def _final_norm(x, g, *, tm):
    m, d = x.shape
    return pl.pallas_call(
        _final_norm_kernel,
        grid=(m // tm,),
        in_specs=[pl.BlockSpec((tm, d), lambda i: (i, 0)), pl.BlockSpec((1, d), lambda i: (0, 0))],
        out_specs=pl.BlockSpec((tm, d), lambda i: (i, 0)),
        out_shape=jax.ShapeDtypeStruct((m, d), _F32),
        compiler_params=_params("parallel"),
    )(x, g.reshape(1, d))


def _rope_tables(seq):
    inv_freq = 1.0 / (ROPE_THETA ** (jnp.arange(0, HEAD_DIM, 2, dtype=_F32) / HEAD_DIM))
    ang = jnp.arange(seq, dtype=_F32)[:, None] * inv_freq[None, :]
    cos, sin = jnp.cos(ang), jnp.sin(ang)
    return jnp.concatenate([cos, cos], axis=-1), jnp.concatenate([-sin, sin], axis=-1)


def kernel(x, mem, g_mix, w_in, g_out_a, g_out_b, g_out_c, w_out, g_cross, g_mem,
           w_cq, w_ckv, w_co, g_mlp, w_up, w_down, g_final):
    bsz, seq, d = x.shape
    depth = w_in.shape[0]
    mem_len = mem.shape[1]
    m = bsz * seq
    tm = min(512, seq)
    width = N_MIX_HEADS * HEAD_DIM
    cos2, sin2 = _rope_tables(seq)
    w_in_b, w_out_b, w_cq_b, w_ckv_b, w_co_b, w_up_b, w_down_b = (
        w.astype(_BF16) for w in (w_in, w_out, w_cq, w_ckv, w_co, w_up, w_down))
    g_out = jnp.concatenate([g_out_a, g_out_b, g_out_c], axis=-1)

    xf = x.reshape(m, d)
    memf = mem.reshape(bsz * mem_len, d)
    for l in range(depth):
        wq, wk, wv = (w_in_b[l, :, i * width:(i + 1) * width] for i in range(3))
        q = _in_proj(xf, g_mix[l], wq, tm=tm, seq=seq, scale=QK_SCALE, rope=(cos2 * QK_SCALE, sin2 * QK_SCALE))
        k = _in_proj(xf, g_mix[l], wk, tm=tm, seq=seq, rope=(cos2, sin2))
        v = _in_proj(xf, g_mix[l], wv, tm=tm, seq=seq)
        q, k, v = (a.reshape(bsz, seq, width) for a in (q, k, v))
        parts = [mix(q, k, v, bsz=bsz, seq=seq).reshape(m, -1) for mix in (_moba, _stick_breaking, _dilated)]
        xf = _out_proj(parts, g_out[l], w_out_b[l], xf, tm=tm)
        kv = _norm_matmul(memf, g_mem, w_ckv_b[l], tm=min(512, bsz * mem_len), tn=512,
                          out_dtype=_BF16).reshape(bsz, mem_len, -1)
        xf = _cross(xf, g_cross[l], w_cq_b[l], kv, w_co_b[l], seq=seq, tm=tm)
        xf = _mlp(xf, g_mlp[l], w_up_b[l], w_down_b[l], tm=min(1024, seq), tf=512)
    return _final_norm(xf, g_final, tm=tm).reshape(bsz, seq, d)
```

```python
import functools

import jax
import jax.numpy as jnp
from jax import lax
from jax.experimental import pallas as pl
from jax.experimental.pallas import tpu as pltpu

HEAD_DIM = 128
N_MIX_HEADS = 16
N_HEADS_A = 4
N_HEADS_B = 6
N_HEADS_C = 6
DILATIONS = (1, 4, 16)
BAND = 128
C_HEADS_PER_PATTERN = N_HEADS_C // len(DILATIONS)
MOBA_BLOCK = 256
MOBA_TOPK = 3
SB_BLOCK = 256
BLOCKS_PER_STEP = 2
HEADS_PER_STEP = 2
DILATED_UNITS_PER_STEP = 8
ROPE_THETA = 10000.0
CROSS_HEADS = 4
CROSS_HEAD_DIM = 128
RMS_EPS = 1e-6
NEG_INF = -1e30
REMOVED = -3e38
EXP_UNDERFLOW = -104.0
QK_SCALE = HEAD_DIM ** -0.5

LANES = 128
V7X_VMEM_LIMIT_BYTES = 56 * 1024 * 1024

_BF16 = jnp.bfloat16
_F32 = jnp.float32

HEAD_A0, HEAD_B0, HEAD_C0 = 0, N_HEADS_A, N_HEADS_A + N_HEADS_B


def _dot(a, b):
    return jnp.dot(a, b, preferred_element_type=_F32)


def _dot_nt(a, b):
    return lax.dot_general(a, b, (((1,), (1,)), ((), ())), preferred_element_type=_F32)


def _rms(x, g):
    ms = jnp.mean(x * x, axis=-1, keepdims=True)
    return x * lax.rsqrt(ms + RMS_EPS) * g


def _params(*semantics):
    return pltpu.CompilerParams(dimension_semantics=semantics, vmem_limit_bytes=V7X_VMEM_LIMIT_BYTES)


def _norm_matmul_kernel(x_ref, g_ref, w_ref, o_ref, hn_ref):
    @pl.when(pl.program_id(1) == 0)
    def _normalise():
        hn_ref[...] = _rms(x_ref[...], g_ref[...]).astype(_BF16)

    o_ref[...] = _dot(hn_ref[...], w_ref[...]).astype(o_ref.dtype)


def _norm_matmul(x, g, w, layer, *, tm, tn, out_dtype):
    m, k = x.shape
    n = w.shape[2]
    assert m % tm == 0 and n % tn == 0
    return pl.pallas_call(
        _norm_matmul_kernel,
        grid=(m // tm, n // tn),
        in_specs=[
            pl.BlockSpec((tm, k), lambda i, j: (i, 0)),
            pl.BlockSpec((1, k), lambda i, j: (0, 0)),
            pl.BlockSpec((None, k, tn), lambda i, j: (layer, 0, j)),
        ],
        out_specs=pl.BlockSpec((tm, tn), lambda i, j: (i, j)),
        out_shape=jax.ShapeDtypeStruct((m, n), out_dtype),
        scratch_shapes=[pltpu.VMEM((tm, k), _BF16)],
        compiler_params=_params("parallel", "arbitrary"),
    )(x, g.reshape(1, k), w)


def _in_proj_kernel(*refs, scale, rotary_heads):
    if any(rotary_heads):
        x_ref, g_ref, w_ref, cos_ref, sin_ref, o_ref = refs
        c, s = cos_ref[...], sin_ref[...]
    else:
        x_ref, g_ref, w_ref, o_ref = refs
    acc = _dot(_rms(x_ref[...], g_ref[...]).astype(_BF16), w_ref[...])
    for h, rotary in enumerate(rotary_heads):
        sl = slice(h * LANES, (h + 1) * LANES)
        a = acc[:, sl]
        if rotary:
            a = a * c + pltpu.roll(a, LANES // 2, 1) * s
        elif scale != 1.0:
            a = a * scale
        o_ref[:, sl] = a.astype(o_ref.dtype)


def _in_proj(x, g, w, layer, part, *, tm, seq, scale=1.0, rope=None):
    m, k = x.shape
    n = N_MIX_HEADS * HEAD_DIM
    assert m % tm == 0 and seq % tm == 0 and w.shape[2] == 3 * n
    rotary_heads = tuple(rope is not None and not HEAD_B0 <= h < HEAD_C0 for h in range(N_MIX_HEADS))
    tiles_per_seq = seq // tm
    operands = [x, g.reshape(1, k), w]
    in_specs = [
        pl.BlockSpec((tm, k), lambda i: (i, 0)),
        pl.BlockSpec((1, k), lambda i: (0, 0)),
        pl.BlockSpec((None, k, n), lambda i: (layer, 0, part)),
    ]
    if rope is not None:
        operands += list(rope)
        in_specs += [pl.BlockSpec((tm, LANES), lambda i: (i % tiles_per_seq, 0))] * 2
    return pl.pallas_call(
        functools.partial(_in_proj_kernel, scale=scale, rotary_heads=rotary_heads),
        grid=(m // tm,),
        in_specs=in_specs,
        out_specs=pl.BlockSpec((tm, n), lambda i: (i, 0)),
        out_shape=jax.ShapeDtypeStruct((m, n), _BF16),
        compiler_params=_params("parallel"),
    )(*operands)


def _out_proj_kernel(a_ref, b_ref, c_ref, g_ref, w_ref, x_ref, o_ref):
    acc = x_ref[...]
    off = 0
    for ref in (a_ref, b_ref, c_ref):
        width = ref.shape[1]
        hn = _rms(ref[...], g_ref[:, off:off + width]).astype(_BF16)
        acc = acc + _dot(hn, w_ref[off:off + width, :])
        off += width
    o_ref[...] = acc


def _out_proj(parts, g, w, layer, x, *, tm):
    m, d = x.shape
    widths = [p.shape[1] for p in parts]
    assert sum(widths) == w.shape[1] and m % tm == 0
    return pl.pallas_call(
        _out_proj_kernel,
        grid=(m // tm,),
        in_specs=[pl.BlockSpec((tm, wd), lambda i: (i, 0)) for wd in widths] + [
            pl.BlockSpec((1, w.shape[1]), lambda i: (0, 0)),
            pl.BlockSpec((None,) + w.shape[1:], lambda i: (layer, 0, 0)),
            pl.BlockSpec((tm, d), lambda i: (i, 0)),
        ],
        out_specs=pl.BlockSpec((tm, d), lambda i: (i, 0)),
        out_shape=jax.ShapeDtypeStruct((m, d), _F32),
        compiler_params=_params("parallel"),
    )(*parts, g.reshape(1, -1), w, x)


def _mlp_kernel(x_ref, g_ref, wu_ref, wd_ref, o_ref, hn_ref):
    @pl.when(pl.program_id(1) == 0)
    def _start():
        x = x_ref[...]
        hn_ref[...] = _rms(x, g_ref[...]).astype(_BF16)
        o_ref[...] = x

    u = _dot(hn_ref[...], wu_ref[...])
    r = jnp.square(jnp.maximum(u, 0.0)).astype(_BF16)
    o_ref[...] += _dot(r, wd_ref[...])


def _mlp(x, g, w_up, w_down, layer, *, tm, tf):
    m, d = x.shape
    f = w_up.shape[2]
    assert m % tm == 0 and f % tf == 0
    return pl.pallas_call(
        _mlp_kernel,
        grid=(m // tm, f // tf),
        in_specs=[
            pl.BlockSpec((tm, d), lambda i, j: (i, 0)),
            pl.BlockSpec((1, d), lambda i, j: (0, 0)),
            pl.BlockSpec((None, d, tf), lambda i, j: (layer, 0, j)),
            pl.BlockSpec((None, tf, d), lambda i, j: (layer, j, 0)),
        ],
        out_specs=pl.BlockSpec((tm, d), lambda i, j: (i, 0)),
        out_shape=jax.ShapeDtypeStruct((m, d), _F32),
        scratch_shapes=[pltpu.VMEM((tm, d), _BF16)],
        compiler_params=_params("parallel", "arbitrary"),
    )(x, g.reshape(1, d), w_up, w_down)


def _cross_kernel(x_ref, g_ref, wq_ref, kv_ref, wo_ref, o_ref):
    x = x_ref[...]
    hn = _rms(x, g_ref[...]).astype(_BF16)
    q = _dot(hn, wq_ref[...]).astype(_BF16)
    kv = kv_ref[0]
    width = CROSS_HEADS * CROSS_HEAD_DIM
    outs = []
    for h in range(CROSS_HEADS):
        sl = slice(h * CROSS_HEAD_DIM, (h + 1) * CROSS_HEAD_DIM)
        s = _dot_nt(q[:, sl], kv[:, sl]) * (CROSS_HEAD_DIM ** -0.5)
        e = jnp.exp(s - jnp.max(s, axis=-1, keepdims=True))
        p = e / jnp.sum(e, axis=-1, keepdims=True)
        outs.append(_dot(p.astype(_BF16), kv[:, width + h * CROSS_HEAD_DIM: width + (h + 1) * CROSS_HEAD_DIM]))
    o = jnp.concatenate(outs, axis=1).astype(_BF16)
    o_ref[...] = x + _dot(o, wo_ref[...])


def _cross(x, g, w_q, kv, w_o, layer, *, seq, tm):
    m, d = x.shape
    width = w_q.shape[2]
    mem_len = kv.shape[1]
    assert seq % tm == 0
    tiles_per_seq = seq // tm
    return pl.pallas_call(
        _cross_kernel,
        grid=(m // tm,),
        in_specs=[
            pl.BlockSpec((tm, d), lambda i: (i, 0)),
            pl.BlockSpec((1, d), lambda i: (0, 0)),
            pl.BlockSpec((None, d, width), lambda i: (layer, 0, 0)),
            pl.BlockSpec((1, mem_len, 2 * width), lambda i: (i // tiles_per_seq, 0, 0)),
            pl.BlockSpec((None, width, d), lambda i: (layer, 0, 0)),
        ],
        out_specs=pl.BlockSpec((tm, d), lambda i: (i, 0)),
        out_shape=jax.ShapeDtypeStruct((m, d), _F32),
        compiler_params=_params("parallel"),
    )(x, g.reshape(1, d), w_q, kv, w_o)


def _moba_kernel(q_ref, k_ref, v_ref, o_ref, km_ref, onehot_ref, *, n_blk):
    qi = pl.program_id(2)
    blk = MOBA_BLOCK
    kb = BLOCKS_PER_STEP * blk
    heads = q_ref.shape[2] // LANES

    @pl.when(qi == 0)
    def _prepare():
        km_ref[...] = jnp.zeros_like(km_ref)
        for h in range(heads):
            for j in range(n_blk):
                rows = k_ref[0, j * blk:(j + 1) * blk, h * LANES:(h + 1) * LANES].astype(_F32)
                km_ref[h, j:j + 1, :] = jnp.sum(rows, axis=0, keepdims=True) * (1.0 / blk)
        key = lax.broadcasted_iota(jnp.int32, onehot_ref.shape, 0)
        lane = lax.broadcasted_iota(jnp.int32, onehot_ref.shape, 1)
        onehot_ref[...] = ((key >= lane * blk) & (key < (lane + 1) * blk)).astype(_BF16)

    colf = lax.broadcasted_iota(jnp.int32, (kb, LANES), 1).astype(_F32)
    rowl = lax.broadcasted_iota(jnp.int32, (kb, LANES), 0)
    row = lax.broadcasted_iota(jnp.int32, (kb, kb), 0)
    col = lax.broadcasted_iota(jnp.int32, (kb, kb), 1)
    own = (qi * BLOCKS_PER_STEP).astype(_F32)
    for p in range(1, BLOCKS_PER_STEP):
        own = own + (rowl >= p * blk).astype(_F32)

    def masked_queries(h):
        q = q_ref[0, :, h * LANES:(h + 1) * LANES]
        km = km_ref[h]
        km_hi = km.astype(_BF16)
        km_lo = (km - km_hi.astype(_F32)).astype(_BF16)
        gate = _dot_nt(q, km_hi) + _dot_nt(q, km_lo)
        g = jnp.where(colf < own, gate, NEG_INF)
        keep = colf == own
        for _ in range(MOBA_TOPK):
            mx = jnp.max(g, axis=-1, keepdims=True)
            idx = jnp.min(jnp.where(g == mx, colf, float(LANES)), axis=-1, keepdims=True)
            taken = colf == idx
            keep = keep | taken
            g = jnp.where(taken, REMOVED, g)
        return jnp.concatenate([q, jnp.where(keep, 0.0, NEG_INF).astype(_BF16)], axis=1)

    def key_blocks(h, sbi, q_aug, carry, diagonal):
        hs = slice(h * LANES, (h + 1) * LANES)
        start = pl.multiple_of(sbi * kb, kb)
        k_aug = jnp.concatenate([k_ref[0, pl.ds(start, kb), hs], onehot_ref[pl.ds(start, kb), :]], axis=1)
        s = _dot_nt(q_aug, k_aug)
        if diagonal:
            s = jnp.where(col <= row, s, NEG_INF)
        mx = jnp.max(s, axis=-1, keepdims=True)
        if carry is None:
            p = jnp.exp(s - mx)
            return mx, jnp.sum(p, axis=-1, keepdims=True), _dot(p.astype(_BF16), v_ref[0, pl.ds(start, kb), hs])
        m, l, acc = carry
        m_new = jnp.maximum(m, mx)
        a = jnp.exp(m - m_new)
        p = jnp.exp(s - m_new)
        l = a * l + jnp.sum(p, axis=-1, keepdims=True)
        acc = a * acc + _dot(p.astype(_BF16), v_ref[0, pl.ds(start, kb), hs])
        return m_new, l, acc

    q_aug = [masked_queries(h) for h in range(heads)]
    carry0 = tuple(key_blocks(h, qi, q_aug[h], None, True) for h in range(heads))

    def past_blocks(it, carry):
        return tuple(key_blocks(h, qi - 1 - it, q_aug[h], carry[h], False) for h in range(heads))

    carry = lax.fori_loop(0, qi, past_blocks, carry0)
    for h in range(heads):
        _, l, acc = carry[h]
        o_ref[0, :, h * LANES:(h + 1) * LANES] = acc / l


def _head_group_specs(seq, q_rows, head0, hp):
    assert head0 % hp == 0
    w = hp * LANES
    return [
        pl.BlockSpec((1, q_rows, w), lambda b, h, i: (b, i, head0 // hp + h)),
        pl.BlockSpec((1, seq, w), lambda b, h, i: (b, 0, head0 // hp + h)),
        pl.BlockSpec((1, seq, w), lambda b, h, i: (b, 0, head0 // hp + h)),
    ]


def _moba(q, k, v, *, bsz, seq):
    n_blk = seq // MOBA_BLOCK
    hp = N_HEADS_A
    tq = MOBA_BLOCK * BLOCKS_PER_STEP
    assert seq % tq == 0 and MOBA_TOPK < n_blk <= LANES and N_HEADS_A % hp == 0
    return pl.pallas_call(
        functools.partial(_moba_kernel, n_blk=n_blk),
        grid=(bsz, N_HEADS_A // hp, seq // tq),
        in_specs=_head_group_specs(seq, tq, HEAD_A0, hp),
        out_specs=pl.BlockSpec((1, tq, hp * LANES), lambda b, h, i: (b, i, h)),
        out_shape=jax.ShapeDtypeStruct((bsz, seq, N_HEADS_A * HEAD_DIM), _F32),
        scratch_shapes=[pltpu.VMEM((hp, LANES, LANES), _F32), pltpu.VMEM((seq, LANES), _BF16)],
        compiler_params=_params("parallel", "parallel", "arbitrary"),
    )(q, k, v)


def _sb_kernel(q_ref, k_ref, v_ref, o_ref):
    qi = pl.program_id(2)
    blk = SB_BLOCK
    kb = BLOCKS_PER_STEP * blk
    heads = q_ref.shape[2] // LANES
    row2 = lax.broadcasted_iota(jnp.int32, (2 * blk, blk), 0)
    col2 = lax.broadcasted_iota(jnp.int32, (2 * blk, blk), 1)
    later = ((row2 > col2) & (row2 < blk) | (row2 - blk > col2)).astype(_BF16)
    past = lax.broadcasted_iota(jnp.int32, (kb, kb), 1) < lax.broadcasted_iota(jnp.int32, (kb, kb), 0)

    def suffix_sums(x):
        hi = x.astype(_BF16)
        lo = (x - hi.astype(_F32)).astype(_BF16)
        return _dot(jnp.concatenate([hi, lo], axis=1), later)

    def key_blocks(h, sbi, carry, diagonal):
        hs = slice(h * LANES, (h + 1) * LANES)
        start = pl.multiple_of(sbi * kb, kb)
        z = _dot_nt(q_ref[0, :, hs], k_ref[0, pl.ds(start, kb), hs])
        log_beta = jnp.minimum(z, 0.0) - jnp.log(1.0 + jnp.exp(-jnp.abs(z)))
        log_keep = log_beta - z
        if diagonal:
            log_keep = jnp.where(past, log_keep, 0.0)
        sticks = []
        tail = None if carry is None else carry[1]
        for p in reversed(range(BLOCKS_PER_STEP)):
            lk = log_keep[:, p * blk:(p + 1) * blk]
            st = suffix_sums(lk)
            sticks.append(st if tail is None else st + tail)
            total = jnp.sum(lk, axis=-1, keepdims=True)
            tail = total if tail is None else tail + total
        w = jnp.exp(log_beta + jnp.concatenate(sticks[::-1], axis=1))
        if diagonal:
            w = jnp.where(past, w, 0.0)
        out = _dot(w.astype(_BF16), v_ref[0, pl.ds(start, kb), hs])
        return (out if carry is None else carry[0] + out), tail

    def weights_alive(carry):
        worst = functools.reduce(jnp.maximum, [jnp.max(c[1]) for c in carry])
        return (worst >= EXP_UNDERFLOW).astype(jnp.int32)

    carry0 = tuple(key_blocks(h, qi, None, True) for h in range(heads))

    def more_blocks(state):
        it, alive, _ = state
        return (it < qi) & (alive > 0)

    def past_blocks(state):
        it, _, carry = state
        carry = tuple(key_blocks(h, qi - 1 - it, carry[h], False) for h in range(heads))
        return it + 1, weights_alive(carry), carry

    _, _, carry = lax.while_loop(more_blocks, past_blocks, (jnp.int32(0), weights_alive(carry0), carry0))
    for h in range(heads):
        o_ref[0, :, h * LANES:(h + 1) * LANES] = carry[h][0]


def _stick_breaking(q, k, v, *, bsz, seq):
    hp = HEADS_PER_STEP
    tq = SB_BLOCK * BLOCKS_PER_STEP
    assert seq % tq == 0 and N_HEADS_B % hp == 0
    return pl.pallas_call(
        _sb_kernel,
        grid=(bsz, N_HEADS_B // hp, seq // tq),
        in_specs=_head_group_specs(seq, tq, HEAD_B0, hp),
        out_specs=pl.BlockSpec((1, tq, hp * LANES), lambda b, h, i: (b, i, h)),
        out_shape=jax.ShapeDtypeStruct((bsz, seq, N_HEADS_B * HEAD_DIM), _F32),
        compiler_params=_params("parallel", "parallel", "parallel"),
    )(q, k, v)


def _dilated_kernel(q_ref, k_ref, v_ref, o_ref, qf, kf, vf, og, lg, *, seq):
    g = pl.program_id(2)
    qf[...] = q_ref[0].astype(_F32)
    kf[...] = k_ref[0].astype(_F32)
    vf[...] = v_ref[0].astype(_F32)
    row2 = lax.broadcasted_iota(jnp.int32, (BAND, 2 * BAND), 0)
    col2 = lax.broadcasted_iota(jnp.int32, (BAND, 2 * BAND), 1)
    own_block = (col2 >= BAND) & (col2 - BAND <= row2)
    prev_block = (col2 < BAND) & (col2 >= row2)

    def rows(ref, start, d):
        return ref[pl.ds(start, BAND, stride=d), :] if d > 1 else ref[pl.ds(start, BAND), :]

    def unit(gi, d, r, nb):
        q_start = r + nb * (BAND * d)
        p_start = q_start - jnp.where(nb > 0, BAND * d, 0)
        qu = rows(qf, q_start, d).astype(_BF16)
        ku = jnp.concatenate([rows(kf, p_start, d), rows(kf, q_start, d)], axis=0).astype(_BF16)
        vu = jnp.concatenate([rows(vf, p_start, d), rows(vf, q_start, d)], axis=0).astype(_BF16)
        s = jnp.where(own_block | (prev_block & (nb > 0)), _dot_nt(qu, ku), NEG_INF)
        mx = jnp.max(s, axis=-1, keepdims=True)
        e = jnp.exp(s - mx)
        den = jnp.sum(e, axis=-1, keepdims=True)
        out = _dot((e / den).astype(_BF16), vu)
        lse = mx + jnp.log(den)
        dst = pl.ds(q_start, BAND, stride=d) if d > 1 else pl.ds(q_start, BAND)
        og[gi, dst, :] = out
        lg[gi, dst, :] = jnp.broadcast_to(lse, (BAND, LANES))

    def pattern(gi, d):
        def unit_group(i, _):
            for u in range(DILATED_UNITS_PER_STEP):
                idx = i * DILATED_UNITS_PER_STEP + u
                if d == 1:
                    unit(gi, d, 0, idx)
                else:
                    unit(gi, d, lax.rem(idx, d), lax.div(idx, d))
            return 0

        lax.fori_loop(0, seq // (BAND * DILATED_UNITS_PER_STEP), unit_group, 0)

    for gi, d in enumerate(DILATIONS):
        pl.when(g == gi)(functools.partial(pattern, gi, d))

    @pl.when(g == len(DILATIONS) - 1)
    def _mix():
        chunk = 256

        def mix_chunk(c, _):
            sl = pl.ds(pl.multiple_of(c * chunk, chunk), chunk)
            l = [lg[gi, sl, :] for gi in range(len(DILATIONS))]
            mx = jnp.maximum(jnp.maximum(l[0], l[1]), l[2])
            e = [jnp.exp(li - mx) for li in l]
            den = e[0] + e[1] + e[2]
            o_ref[0, sl, :] = (e[0] / den) * og[0, sl, :] + (e[1] / den) * og[1, sl, :] + (e[2] / den) * og[2, sl, :]
            return 0

        lax.fori_loop(0, seq // chunk, mix_chunk, 0)


def _dilated(q, k, v, *, bsz, seq):
    assert seq % (BAND * max(DILATIONS)) == 0 and seq % 256 == 0
    n_pat = len(DILATIONS)
    per = C_HEADS_PER_PATTERN
    spec = pl.BlockSpec((1, seq, LANES), lambda b, j, g: (b, 0, HEAD_C0 + g * per + j))
    return pl.pallas_call(
        functools.partial(_dilated_kernel, seq=seq),
        grid=(bsz, per, n_pat),
        in_specs=[spec, spec, spec],
        out_specs=pl.BlockSpec((1, seq, LANES), lambda b, j, g: (b, 0, j)),
        out_shape=jax.ShapeDtypeStruct((bsz, seq, per * HEAD_DIM), _F32),
        scratch_shapes=[pltpu.VMEM((seq, LANES), _F32)] * 3 + [pltpu.VMEM((n_pat, seq, LANES), _F32)] * 2,
        compiler_params=_params("parallel", "parallel", "arbitrary"),
    )(q, k, v)


def _final_norm_kernel(x_ref, g_ref, o_ref):
    o_ref[...] = _rms(x_ref[...], g_ref[...])


def _final_norm(x, g, *, tm):
    m, d = x.shape
    return pl.pallas_call(
        _final_norm_kernel,
        grid=(m // tm,),
        in_specs=[pl.BlockSpec((tm, d), lambda i: (i, 0)), pl.BlockSpec((1, d), lambda i: (0, 0))],
        out_specs=pl.BlockSpec((tm, d), lambda i: (i, 0)),
        out_shape=jax.ShapeDtypeStruct((m, d), _F32),
        compiler_params=_params("parallel"),
    )(x, g.reshape(1, d))


def _rope_tables(seq):
    inv_freq = 1.0 / (ROPE_THETA ** (jnp.arange(0, HEAD_DIM, 2, dtype=_F32) / HEAD_DIM))
    ang = jnp.arange(seq, dtype=_F32)[:, None] * inv_freq[None, :]
    cos, sin = jnp.cos(ang), jnp.sin(ang)
    return jnp.concatenate([cos, cos], axis=-1), jnp.concatenate([-sin, sin], axis=-1)


def kernel(x, mem, g_mix, w_in, g_out_a, g_out_b, g_out_c, w_out, g_cross, g_mem,
           w_cq, w_ckv, w_co, g_mlp, w_up, w_down, g_final):
    bsz, seq, d = x.shape
    depth = w_in.shape[0]
    mem_len = mem.shape[1]
    m = bsz * seq
    tm = min(512, seq)
    width = N_MIX_HEADS * HEAD_DIM
    cos2, sin2 = _rope_tables(seq)
    w_in_b, w_out_b, w_cq_b, w_ckv_b, w_co_b, w_up_b, w_down_b = (
        w.astype(_BF16) for w in (w_in, w_out, w_cq, w_ckv, w_co, w_up, w_down))
    g_out = jnp.concatenate([g_out_a, g_out_b, g_out_c], axis=-1)

    xf = x.reshape(m, d)
    memf = mem.reshape(bsz * mem_len, d)
    for l in range(depth):
        q = _in_proj(xf, g_mix[l], w_in_b, l, 0, tm=tm, seq=seq, scale=QK_SCALE,
                     rope=(cos2 * QK_SCALE, sin2 * QK_SCALE))
        k = _in_proj(xf, g_mix[l], w_in_b, l, 1, tm=tm, seq=seq, rope=(cos2, sin2))
        v = _in_proj(xf, g_mix[l], w_in_b, l, 2, tm=tm, seq=seq)
        q, k, v = (a.reshape(bsz, seq, width) for a in (q, k, v))
        parts = [mix(q, k, v, bsz=bsz, seq=seq).reshape(m, -1) for mix in (_moba, _stick_breaking, _dilated)]
        xf = _out_proj(parts, g_out[l], w_out_b, l, xf, tm=tm)
        kv = _norm_matmul(memf, g_mem, w_ckv_b, l, tm=min(512, bsz * mem_len), tn=512,
                          out_dtype=_BF16).reshape(bsz, mem_len, -1)
        xf = _cross(xf, g_cross[l], w_cq_b, kv, w_co_b, l, seq=seq, tm=tm)
        xf = _mlp(xf, g_mlp[l], w_up_b, w_down_b, l, tm=min(1024, seq), tf=512)
    return _final_norm(xf, g_final, tm=tm).reshape(bsz, seq, d)
```

```python
import functools

import jax
import jax.numpy as jnp
from jax import lax
from jax.experimental import pallas as pl
from jax.experimental.pallas import tpu as pltpu

HEAD_DIM = 128
N_MIX_HEADS = 16
N_HEADS_A = 4
N_HEADS_B = 6
N_HEADS_C = 6
DILATIONS = (1, 4, 16)
BAND = 128
C_HEADS_PER_PATTERN = N_HEADS_C // len(DILATIONS)
MOBA_BLOCK = 256
MOBA_TOPK = 3
SB_BLOCK = 256
BLOCKS_PER_STEP = 2
HEADS_PER_STEP = 2
DILATED_UNITS_PER_STEP = 8
ROPE_THETA = 10000.0
CROSS_HEADS = 4
CROSS_HEAD_DIM = 128
RMS_EPS = 1e-6
NEG_INF = -1e30
REMOVED = -3e38
EXP_UNDERFLOW = -104.0
QK_SCALE = HEAD_DIM ** -0.5

LANES = 128
V7X_VMEM_LIMIT_BYTES = 56 * 1024 * 1024

_BF16 = jnp.bfloat16
_F32 = jnp.float32

HEAD_A0, HEAD_B0, HEAD_C0 = 0, N_HEADS_A, N_HEADS_A + N_HEADS_B


def _dot(a, b):
    return jnp.dot(a, b, preferred_element_type=_F32)


def _dot_nt(a, b):
    return lax.dot_general(a, b, (((1,), (1,)), ((), ())), preferred_element_type=_F32)


def _rms(x, g):
    ms = jnp.mean(x * x, axis=-1, keepdims=True)
    return x * lax.rsqrt(ms + RMS_EPS) * g


def _params(*semantics):
    return pltpu.CompilerParams(dimension_semantics=semantics, vmem_limit_bytes=V7X_VMEM_LIMIT_BYTES)


def _norm_matmul_kernel(x_ref, g_ref, w_ref, o_ref, hn_ref):
    @pl.when(pl.program_id(1) == 0)
    def _normalise():
        hn_ref[...] = _rms(x_ref[...], g_ref[...]).astype(_BF16)

    o_ref[...] = _dot(hn_ref[...], w_ref[...]).astype(o_ref.dtype)


def _norm_matmul(x, g, w, layer, *, tm, tn, out_dtype):
    m, k = x.shape
    n = w.shape[2]
    assert m % tm == 0 and n % tn == 0
    return pl.pallas_call(
        _norm_matmul_kernel,
        grid=(m // tm, n // tn),
        in_specs=[
            pl.BlockSpec((tm, k), lambda i, j: (i, 0)),
            pl.BlockSpec((1, k), lambda i, j: (0, 0)),
            pl.BlockSpec((None, k, tn), lambda i, j: (layer, 0, j)),
        ],
        out_specs=pl.BlockSpec((tm, tn), lambda i, j: (i, j)),
        out_shape=jax.ShapeDtypeStruct((m, n), out_dtype),
        scratch_shapes=[pltpu.VMEM((tm, k), _BF16)],
        compiler_params=_params("parallel", "arbitrary"),
    )(x, g.reshape(1, k), w)


def _in_proj_kernel(*refs, scale, rotary_heads):
    if any(rotary_heads):
        x_ref, g_ref, w_ref, cos_ref, sin_ref, o_ref = refs
        c, s = cos_ref[...], sin_ref[...]
    else:
        x_ref, g_ref, w_ref, o_ref = refs
    acc = _dot(_rms(x_ref[...], g_ref[...]).astype(_BF16), w_ref[...])
    for h, rotary in enumerate(rotary_heads):
        sl = slice(h * LANES, (h + 1) * LANES)
        a = acc[:, sl]
        if rotary:
            a = a * c + pltpu.roll(a, LANES // 2, 1) * s
        elif scale != 1.0:
            a = a * scale
        o_ref[:, sl] = a.astype(o_ref.dtype)


def _in_proj(x, g, w, layer, part, *, tm, seq, scale=1.0, rope=None):
    m, k = x.shape
    n = N_MIX_HEADS * HEAD_DIM
    assert m % tm == 0 and seq % tm == 0 and w.shape[2] == 3 * n
    rotary_heads = tuple(rope is not None and not HEAD_B0 <= h < HEAD_C0 for h in range(N_MIX_HEADS))
    tiles_per_seq = seq // tm
    operands = [x, g.reshape(1, k), w]
    in_specs = [
        pl.BlockSpec((tm, k), lambda i: (i, 0)),
        pl.BlockSpec((1, k), lambda i: (0, 0)),
        pl.BlockSpec((None, k, n), lambda i: (layer, 0, part)),
    ]
    if rope is not None:
        operands += list(rope)
        in_specs += [pl.BlockSpec((tm, LANES), lambda i: (i % tiles_per_seq, 0))] * 2
    return pl.pallas_call(
        functools.partial(_in_proj_kernel, scale=scale, rotary_heads=rotary_heads),
        grid=(m // tm,),
        in_specs=in_specs,
        out_specs=pl.BlockSpec((tm, n), lambda i: (i, 0)),
        out_shape=jax.ShapeDtypeStruct((m, n), _BF16),
        compiler_params=_params("parallel"),
    )(*operands)


def _out_cross_kernel(a_ref, b_ref, c_ref, go_ref, wout_ref, x_ref, gc_ref, wq_ref, kv_ref, wo_ref, o_ref):
    x = x_ref[...]
    off = 0
    for ref in (a_ref, b_ref, c_ref):
        width = ref.shape[1]
        hn = _rms(ref[...], go_ref[:, off:off + width]).astype(_BF16)
        x = x + _dot(hn, wout_ref[off:off + width, :])
        off += width
    q = _dot(_rms(x, gc_ref[...]).astype(_BF16), wq_ref[...]).astype(_BF16)
    kv = kv_ref[0]
    width = CROSS_HEADS * CROSS_HEAD_DIM
    outs = []
    for h in range(CROSS_HEADS):
        sl = slice(h * CROSS_HEAD_DIM, (h + 1) * CROSS_HEAD_DIM)
        s = _dot_nt(q[:, sl], kv[:, sl]) * (CROSS_HEAD_DIM ** -0.5)
        e = jnp.exp(s - jnp.max(s, axis=-1, keepdims=True))
        p = e / jnp.sum(e, axis=-1, keepdims=True)
        outs.append(_dot(p.astype(_BF16), kv[:, width + h * CROSS_HEAD_DIM: width + (h + 1) * CROSS_HEAD_DIM]))
    o_ref[...] = x + _dot(jnp.concatenate(outs, axis=1).astype(_BF16), wo_ref[...])


def _out_cross(parts, g_out, w_out, x, g_cross, w_q, kv, w_o, layer, *, seq, tm):
    m, d = x.shape
    widths = [p.shape[1] for p in parts]
    width = w_q.shape[2]
    mem_len = kv.shape[1]
    assert sum(widths) == w_out.shape[1] and m % tm == 0 and seq % tm == 0
    tiles_per_seq = seq // tm
    return pl.pallas_call(
        _out_cross_kernel,
        grid=(m // tm,),
        in_specs=[pl.BlockSpec((tm, wd), lambda i: (i, 0)) for wd in widths] + [
            pl.BlockSpec((1, w_out.shape[1]), lambda i: (0, 0)),
            pl.BlockSpec((None,) + w_out.shape[1:], lambda i: (layer, 0, 0)),
            pl.BlockSpec((tm, d), lambda i: (i, 0)),
            pl.BlockSpec((1, d), lambda i: (0, 0)),
            pl.BlockSpec((None, d, width), lambda i: (layer, 0, 0)),
            pl.BlockSpec((1, mem_len, 2 * width), lambda i: (i // tiles_per_seq, 0, 0)),
            pl.BlockSpec((None, width, d), lambda i: (layer, 0, 0)),
        ],
        out_specs=pl.BlockSpec((tm, d), lambda i: (i, 0)),
        out_shape=jax.ShapeDtypeStruct((m, d), _F32),
        compiler_params=_params("parallel"),
    )(*parts, g_out.reshape(1, -1), w_out, x, g_cross.reshape(1, d), w_q, kv, w_o)


def _mlp_kernel(*refs, final_norm):
    if final_norm:
        x_ref, g_ref, wu_ref, wd_ref, gf_ref, o_ref, hn_ref = refs
    else:
        x_ref, g_ref, wu_ref, wd_ref, o_ref, hn_ref = refs

    @pl.when(pl.program_id(1) == 0)
    def _start():
        x = x_ref[...]
        hn_ref[...] = _rms(x, g_ref[...]).astype(_BF16)
        o_ref[...] = x

    u = _dot(hn_ref[...], wu_ref[...])
    r = jnp.square(jnp.maximum(u, 0.0)).astype(_BF16)
    o_ref[...] += _dot(r, wd_ref[...])

    if final_norm:
        @pl.when(pl.program_id(1) == pl.num_programs(1) - 1)
        def _finish():
            o_ref[...] = _rms(o_ref[...], gf_ref[...])


def _mlp(x, g, w_up, w_down, layer, *, tm, tf, g_final=None):
    m, d = x.shape
    f = w_up.shape[2]
    assert m % tm == 0 and f % tf == 0
    operands = [x, g.reshape(1, d), w_up, w_down]
    in_specs = [
        pl.BlockSpec((tm, d), lambda i, j: (i, 0)),
        pl.BlockSpec((1, d), lambda i, j: (0, 0)),
        pl.BlockSpec((None, d, tf), lambda i, j: (layer, 0, j)),
        pl.BlockSpec((None, tf, d), lambda i, j: (layer, j, 0)),
    ]
    if g_final is not None:
        operands.append(g_final.reshape(1, d))
        in_specs.append(pl.BlockSpec((1, d), lambda i, j: (0, 0)))
    return pl.pallas_call(
        functools.partial(_mlp_kernel, final_norm=g_final is not None),
        grid=(m // tm, f // tf),
        in_specs=in_specs,
        out_specs=pl.BlockSpec((tm, d), lambda i, j: (i, 0)),
        out_shape=jax.ShapeDtypeStruct((m, d), _F32),
        scratch_shapes=[pltpu.VMEM((tm, d), _BF16)],
        compiler_params=_params("parallel", "arbitrary"),
    )(*operands)


def _moba_kernel(q_ref, k_ref, v_ref, o_ref, km_ref, onehot_ref, *, n_blk):
    qi = pl.program_id(2)
    blk = MOBA_BLOCK
    kb = BLOCKS_PER_STEP * blk
    heads = q_ref.shape[2] // LANES

    @pl.when(qi == 0)
    def _prepare():
        km_ref[...] = jnp.zeros_like(km_ref)
        for h in range(heads):
            for j in range(n_blk):
                rows = k_ref[0, j * blk:(j + 1) * blk, h * LANES:(h + 1) * LANES].astype(_F32)
                km_ref[h, j:j + 1, :] = jnp.sum(rows, axis=0, keepdims=True) * (1.0 / blk)
        key = lax.broadcasted_iota(jnp.int32, onehot_ref.shape, 0)
        lane = lax.broadcasted_iota(jnp.int32, onehot_ref.shape, 1)
        onehot_ref[...] = ((key >= lane * blk) & (key < (lane + 1) * blk)).astype(_BF16)

    colf = lax.broadcasted_iota(jnp.int32, (kb, LANES), 1).astype(_F32)
    rowl = lax.broadcasted_iota(jnp.int32, (kb, LANES), 0)
    row = lax.broadcasted_iota(jnp.int32, (kb, kb), 0)
    col = lax.broadcasted_iota(jnp.int32, (kb, kb), 1)
    own = (qi * BLOCKS_PER_STEP).astype(_F32)
    for p in range(1, BLOCKS_PER_STEP):
        own = own + (rowl >= p * blk).astype(_F32)

    def masked_queries(h):
        q = q_ref[0, :, h * LANES:(h + 1) * LANES]
        km = km_ref[h]
        km_hi = km.astype(_BF16)
        km_lo = (km - km_hi.astype(_F32)).astype(_BF16)
        gate = _dot_nt(q, km_hi) + _dot_nt(q, km_lo)
        g = jnp.where(colf < own, gate, NEG_INF)
        keep = colf == own
        for _ in range(MOBA_TOPK):
            mx = jnp.max(g, axis=-1, keepdims=True)
            idx = jnp.min(jnp.where(g == mx, colf, float(LANES)), axis=-1, keepdims=True)
            taken = colf == idx
            keep = keep | taken
            g = jnp.where(taken, REMOVED, g)
        return jnp.concatenate([q, jnp.where(keep, 0.0, NEG_INF).astype(_BF16)], axis=1)

    def key_blocks(h, sbi, q_aug, carry, diagonal):
        hs = slice(h * LANES, (h + 1) * LANES)
        start = pl.multiple_of(sbi * kb, kb)
        k_aug = jnp.concatenate([k_ref[0, pl.ds(start, kb), hs], onehot_ref[pl.ds(start, kb), :]], axis=1)
        s = _dot_nt(q_aug, k_aug)
        if diagonal:
            s = jnp.where(col <= row, s, NEG_INF)
        mx = jnp.max(s, axis=-1, keepdims=True)
        v_aug = jnp.concatenate([v_ref[0, pl.ds(start, kb), hs], ones], axis=1)
        if carry is None:
            return mx, _dot(jnp.exp(s - mx).astype(_BF16), v_aug)
        m, acc = carry
        m_new = jnp.maximum(m, mx)
        return m_new, jnp.exp(m - m_new) * acc + _dot(jnp.exp(s - m_new).astype(_BF16), v_aug)

    ones = jnp.ones((kb, LANES), _BF16)
    q_aug = [masked_queries(h) for h in range(heads)]
    carry0 = tuple(key_blocks(h, qi, q_aug[h], None, True) for h in range(heads))

    def past_blocks(it, carry):
        return tuple(key_blocks(h, qi - 1 - it, q_aug[h], carry[h], False) for h in range(heads))

    carry = lax.fori_loop(0, qi, past_blocks, carry0)
    for h in range(heads):
        acc = carry[h][1]
        o_ref[0, :, h * LANES:(h + 1) * LANES] = acc[:, :LANES] / acc[:, LANES:]


def _head_group_specs(seq, q_rows, head0, hp):
    assert head0 % hp == 0
    w = hp * LANES
    return [
        pl.BlockSpec((1, q_rows, w), lambda b, h, i: (b, i, head0 // hp + h)),
        pl.BlockSpec((1, seq, w), lambda b, h, i: (b, 0, head0 // hp + h)),
        pl.BlockSpec((1, seq, w), lambda b, h, i: (b, 0, head0 // hp + h)),
    ]


def _moba(q, k, v, *, bsz, seq):
    n_blk = seq // MOBA_BLOCK
    hp = N_HEADS_A
    tq = MOBA_BLOCK * BLOCKS_PER_STEP
    assert seq % tq == 0 and MOBA_TOPK < n_blk <= LANES and N_HEADS_A % hp == 0
    return pl.pallas_call(
        functools.partial(_moba_kernel, n_blk=n_blk),
        grid=(bsz, N_HEADS_A // hp, seq // tq),
        in_specs=_head_group_specs(seq, tq, HEAD_A0, hp),
        out_specs=pl.BlockSpec((1, tq, hp * LANES), lambda b, h, i: (b, i, h)),
        out_shape=jax.ShapeDtypeStruct((bsz, seq, N_HEADS_A * HEAD_DIM), _F32),
        scratch_shapes=[pltpu.VMEM((hp, LANES, LANES), _F32), pltpu.VMEM((seq, LANES), _BF16)],
        compiler_params=_params("parallel", "parallel", "arbitrary"),
    )(q, k, v)


def _sb_kernel(q_ref, k_ref, v_ref, o_ref):
    qi = pl.program_id(2)
    blk = SB_BLOCK
    kb = BLOCKS_PER_STEP * blk
    heads = q_ref.shape[2] // LANES
    row2 = lax.broadcasted_iota(jnp.int32, (2 * blk, blk), 0)
    col2 = lax.broadcasted_iota(jnp.int32, (2 * blk, blk), 1)
    later = ((row2 > col2) & (row2 < blk) | (row2 - blk > col2)).astype(_BF16)
    past = lax.broadcasted_iota(jnp.int32, (kb, kb), 1) < lax.broadcasted_iota(jnp.int32, (kb, kb), 0)

    def suffix_sums(x):
        hi = x.astype(_BF16)
        lo = (x - hi.astype(_F32)).astype(_BF16)
        return _dot(jnp.concatenate([hi, lo], axis=1), later)

    def key_blocks(h, sbi, carry, diagonal):
        hs = slice(h * LANES, (h + 1) * LANES)
        start = pl.multiple_of(sbi * kb, kb)
        z = _dot_nt(q_ref[0, :, hs], k_ref[0, pl.ds(start, kb), hs])
        log_beta = jnp.minimum(z, 0.0) - jnp.log(1.0 + jnp.exp(-jnp.abs(z)))
        log_keep = log_beta - z
        if diagonal:
            log_keep = jnp.where(past, log_keep, 0.0)
        sticks = []
        tail = None if carry is None else carry[1]
        for p in reversed(range(BLOCKS_PER_STEP)):
            lk = log_keep[:, p * blk:(p + 1) * blk]
            st = suffix_sums(lk)
            sticks.append(st if tail is None else st + tail)
            total = jnp.sum(lk, axis=-1, keepdims=True)
            tail = total if tail is None else tail + total
        w = jnp.exp(log_beta + jnp.concatenate(sticks[::-1], axis=1))
        if diagonal:
            w = jnp.where(past, w, 0.0)
        out = _dot(w.astype(_BF16), v_ref[0, pl.ds(start, kb), hs])
        return (out if carry is None else carry[0] + out), tail

    def weights_alive(carry):
        worst = functools.reduce(jnp.maximum, [jnp.max(c[1]) for c in carry])
        return (worst >= EXP_UNDERFLOW).astype(jnp.int32)

    carry0 = tuple(key_blocks(h, qi, None, True) for h in range(heads))

    def more_blocks(state):
        it, alive, _ = state
        return (it < qi) & (alive > 0)

    def past_blocks(state):
        it, _, carry = state
        carry = tuple(key_blocks(h, qi - 1 - it, carry[h], False) for h in range(heads))
        return it + 1, weights_alive(carry), carry

    _, _, carry = lax.while_loop(more_blocks, past_blocks, (jnp.int32(0), weights_alive(carry0), carry0))
    for h in range(heads):
        o_ref[0, :, h * LANES:(h + 1) * LANES] = carry[h][0]


def _stick_breaking(q, k, v, *, bsz, seq):
    hp = HEADS_PER_STEP
    tq = SB_BLOCK * BLOCKS_PER_STEP
    assert seq % tq == 0 and N_HEADS_B % hp == 0
    return pl.pallas_call(
        _sb_kernel,
        grid=(bsz, N_HEADS_B // hp, seq // tq),
        in_specs=_head_group_specs(seq, tq, HEAD_B0, hp),
        out_specs=pl.BlockSpec((1, tq, hp * LANES), lambda b, h, i: (b, i, h)),
        out_shape=jax.ShapeDtypeStruct((bsz, seq, N_HEADS_B * HEAD_DIM), _F32),
        compiler_params=_params("parallel", "parallel", "parallel"),
    )(q, k, v)


def _dilated_kernel(q_ref, k_ref, v_ref, o_ref, qf, kf, vf, og, lg, *, seq):
    g = pl.program_id(2)
    qf[...] = q_ref[0].astype(_F32)
    kf[...] = k_ref[0].astype(_F32)
    vf[...] = v_ref[0].astype(_F32)
    row2 = lax.broadcasted_iota(jnp.int32, (BAND, 2 * BAND), 0)
    col2 = lax.broadcasted_iota(jnp.int32, (BAND, 2 * BAND), 1)
    own_block = (col2 >= BAND) & (col2 - BAND <= row2)
    prev_block = (col2 < BAND) & (col2 >= row2)

    def rows(ref, start, d):
        return ref[pl.ds(start, BAND, stride=d), :] if d > 1 else ref[pl.ds(start, BAND), :]

    def unit(gi, d, r, nb):
        q_start = r + nb * (BAND * d)
        p_start = q_start - jnp.where(nb > 0, BAND * d, 0)
        qu = rows(qf, q_start, d).astype(_BF16)
        ku = jnp.concatenate([rows(kf, p_start, d), rows(kf, q_start, d)], axis=0).astype(_BF16)
        vu = jnp.concatenate([rows(vf, p_start, d), rows(vf, q_start, d)], axis=0).astype(_BF16)
        s = jnp.where(own_block | (prev_block & (nb > 0)), _dot_nt(qu, ku), NEG_INF)
        mx = jnp.max(s, axis=-1, keepdims=True)
        e = jnp.exp(s - mx)
        den = jnp.sum(e, axis=-1, keepdims=True)
        out = _dot((e / den).astype(_BF16), vu)
        lse = mx + jnp.log(den)
        dst = pl.ds(q_start, BAND, stride=d) if d > 1 else pl.ds(q_start, BAND)
        og[gi, dst, :] = out
        lg[gi, dst, :] = jnp.broadcast_to(lse, (BAND, LANES))

    def pattern(gi, d):
        def unit_group(i, _):
            for u in range(DILATED_UNITS_PER_STEP):
                idx = i * DILATED_UNITS_PER_STEP + u
                if d == 1:
                    unit(gi, d, 0, idx)
                else:
                    unit(gi, d, lax.rem(idx, d), lax.div(idx, d))
            return 0

        lax.fori_loop(0, seq // (BAND * DILATED_UNITS_PER_STEP), unit_group, 0)

    for gi, d in enumerate(DILATIONS):
        pl.when(g == gi)(functools.partial(pattern, gi, d))

    @pl.when(g == len(DILATIONS) - 1)
    def _mix():
        chunk = 256

        def mix_chunk(c, _):
            sl = pl.ds(pl.multiple_of(c * chunk, chunk), chunk)
            l = [lg[gi, sl, :] for gi in range(len(DILATIONS))]
            mx = jnp.maximum(jnp.maximum(l[0], l[1]), l[2])
            e = [jnp.exp(li - mx) for li in l]
            den = e[0] + e[1] + e[2]
            o_ref[0, sl, :] = (e[0] / den) * og[0, sl, :] + (e[1] / den) * og[1, sl, :] + (e[2] / den) * og[2, sl, :]
            return 0

        lax.fori_loop(0, seq // chunk, mix_chunk, 0)


def _dilated(q, k, v, *, bsz, seq):
    assert seq % (BAND * max(DILATIONS)) == 0 and seq % 256 == 0
    n_pat = len(DILATIONS)
    per = C_HEADS_PER_PATTERN
    spec = pl.BlockSpec((1, seq, LANES), lambda b, j, g: (b, 0, HEAD_C0 + g * per + j))
    return pl.pallas_call(
        functools.partial(_dilated_kernel, seq=seq),
        grid=(bsz, per, n_pat),
        in_specs=[spec, spec, spec],
        out_specs=pl.BlockSpec((1, seq, LANES), lambda b, j, g: (b, 0, j)),
        out_shape=jax.ShapeDtypeStruct((bsz, seq, per * HEAD_DIM), _F32),
        scratch_shapes=[pltpu.VMEM((seq, LANES), _F32)] * 3 + [pltpu.VMEM((n_pat, seq, LANES), _F32)] * 2,
        compiler_params=_params("parallel", "parallel", "arbitrary"),
    )(q, k, v)


def _rope_tables(seq):
    inv_freq = 1.0 / (ROPE_THETA ** (jnp.arange(0, HEAD_DIM, 2, dtype=_F32) / HEAD_DIM))
    ang = jnp.arange(seq, dtype=_F32)[:, None] * inv_freq[None, :]
    cos, sin = jnp.cos(ang), jnp.sin(ang)
    return jnp.concatenate([cos, cos], axis=-1), jnp.concatenate([-sin, sin], axis=-1)


def kernel(x, mem, g_mix, w_in, g_out_a, g_out_b, g_out_c, w_out, g_cross, g_mem,
           w_cq, w_ckv, w_co, g_mlp, w_up, w_down, g_final):
    bsz, seq, d = x.shape
    depth = w_in.shape[0]
    mem_len = mem.shape[1]
    m = bsz * seq
    tm = min(512, seq)
    tm_in = min(1024, seq)
    width = N_MIX_HEADS * HEAD_DIM
    cos2, sin2 = _rope_tables(seq)
    w_in_b, w_out_b, w_cq_b, w_ckv_b, w_co_b, w_up_b, w_down_b = (
        w.astype(_BF16) for w in (w_in, w_out, w_cq, w_ckv, w_co, w_up, w_down))
    g_out = jnp.concatenate([g_out_a, g_out_b, g_out_c], axis=-1)

    xf = x.reshape(m, d)
    memf = mem.reshape(bsz * mem_len, d)
    for l in range(depth):
        q = _in_proj(xf, g_mix[l], w_in_b, l, 0, tm=tm_in, seq=seq, scale=QK_SCALE,
                     rope=(cos2 * QK_SCALE, sin2 * QK_SCALE))
        k = _in_proj(xf, g_mix[l], w_in_b, l, 1, tm=tm_in, seq=seq, rope=(cos2, sin2))
        v = _in_proj(xf, g_mix[l], w_in_b, l, 2, tm=tm_in, seq=seq)
        q, k, v = (a.reshape(bsz, seq, width) for a in (q, k, v))
        parts = [mix(q, k, v, bsz=bsz, seq=seq).reshape(m, -1) for mix in (_moba, _stick_breaking, _dilated)]
        kv = _norm_matmul(memf, g_mem, w_ckv_b, l, tm=min(512, bsz * mem_len), tn=512,
                          out_dtype=_BF16).reshape(bsz, mem_len, -1)
        xf = _out_cross(parts, g_out[l], w_out_b, xf, g_cross[l], w_cq_b, kv, w_co_b, l, seq=seq, tm=tm)
        xf = _mlp(xf, g_mlp[l], w_up_b, w_down_b, l, tm=min(1024, seq), tf=512,
                  g_final=g_final if l == depth - 1 else None)
    return xf.reshape(bsz, seq, d)
```

```python
import functools

import jax
import jax.numpy as jnp
from jax import lax
from jax.experimental import pallas as pl
from jax.experimental.pallas import tpu as pltpu

HEAD_DIM = 128
N_MIX_HEADS = 16
N_HEADS_A = 4
N_HEADS_B = 6
N_HEADS_C = 6
DILATIONS = (1, 4, 16)
BAND = 128
C_HEADS_PER_PATTERN = N_HEADS_C // len(DILATIONS)
MOBA_BLOCK = 256
MOBA_TOPK = 3
SB_BLOCK = 256
BLOCKS_PER_STEP = 2
SB_BLOCKS_PER_STEP = 2
SB_HEADS_PER_STEP = 3
DILATED_UNITS_PER_STEP = 16
ROPE_THETA = 10000.0
CROSS_HEADS = 4
CROSS_HEAD_DIM = 128
RMS_EPS = 1e-6
NEG_INF = -1e30
EXP_UNDERFLOW = -104.0
QK_SCALE = HEAD_DIM ** -0.5

LANES = 128
V7X_VMEM_LIMIT_BYTES = 56 * 1024 * 1024

_BF16 = jnp.bfloat16
_F32 = jnp.float32

HEAD_A0, HEAD_B0, HEAD_C0 = 0, N_HEADS_A, N_HEADS_A + N_HEADS_B
SLAB_B0, SLAB_C0, SLAB_A0 = 0, N_HEADS_B, N_HEADS_B + N_HEADS_C


def _slab_of_head(h):
    if h < HEAD_B0:
        return SLAB_A0 + h - HEAD_A0
    return SLAB_B0 + h - HEAD_B0 if h < HEAD_C0 else SLAB_C0 + h - HEAD_C0


def _dot(a, b):
    return jnp.dot(a, b, preferred_element_type=_F32)


def _dot_nt(a, b):
    return lax.dot_general(a, b, (((1,), (1,)), ((), ())), preferred_element_type=_F32)


def _rms(x, g):
    ms = jnp.mean(x * x, axis=-1, keepdims=True)
    return x * lax.rsqrt(ms + RMS_EPS) * g


def _params(*semantics):
    return pltpu.CompilerParams(dimension_semantics=semantics, vmem_limit_bytes=V7X_VMEM_LIMIT_BYTES)


def _norm_matmul_kernel(x_ref, g_ref, w_ref, o_ref, hn_ref):
    @pl.when(pl.program_id(1) == 0)
    def _normalise():
        hn_ref[...] = _rms(x_ref[...], g_ref[...]).astype(_BF16)

    o_ref[...] = _dot(hn_ref[...], w_ref[...]).astype(o_ref.dtype)


def _norm_matmul(x, g, w, layer, *, tm, tn, out_dtype):
    m, k = x.shape
    n = w.shape[2]
    assert m % tm == 0 and n % tn == 0
    return pl.pallas_call(
        _norm_matmul_kernel,
        grid=(m // tm, n // tn),
        in_specs=[
            pl.BlockSpec((tm, k), lambda i, j: (i, 0)),
            pl.BlockSpec((1, k), lambda i, j: (0, 0)),
            pl.BlockSpec((None, k, tn), lambda i, j: (layer, 0, j)),
        ],
        out_specs=pl.BlockSpec((tm, tn), lambda i, j: (i, j)),
        out_shape=jax.ShapeDtypeStruct((m, n), out_dtype),
        scratch_shapes=[pltpu.VMEM((tm, k), _BF16)],
        compiler_params=_params("parallel", "arbitrary"),
    )(x, g.reshape(1, k), w)


def _in_proj_kernel(*refs, scale, rotary_heads):
    if any(rotary_heads):
        x_ref, g_ref, w_ref, cos_ref, sin_ref, o_ref = refs
        c, s = cos_ref[...], sin_ref[...]
    else:
        x_ref, g_ref, w_ref, o_ref = refs
    acc = _dot(_rms(x_ref[...], g_ref[...]).astype(_BF16), w_ref[...])
    for h, rotary in enumerate(rotary_heads):
        a = acc[:, h * LANES:(h + 1) * LANES]
        if rotary:
            a = a * c + pltpu.roll(a, LANES // 2, 1) * s
        elif scale != 1.0:
            a = a * scale
        dst = _slab_of_head(h)
        o_ref[:, dst * LANES:(dst + 1) * LANES] = a.astype(o_ref.dtype)


def _in_proj(x, g, w, layer, part, *, tm, seq, scale=1.0, rope=None):
    m, k = x.shape
    n = N_MIX_HEADS * HEAD_DIM
    assert m % tm == 0 and seq % tm == 0 and w.shape[2] == 3 * n
    rotary_heads = tuple(rope is not None and not HEAD_B0 <= h < HEAD_C0 for h in range(N_MIX_HEADS))
    tiles_per_seq = seq // tm
    operands = [x, g.reshape(1, k), w]
    in_specs = [
        pl.BlockSpec((tm, k), lambda i: (i, 0)),
        pl.BlockSpec((1, k), lambda i: (0, 0)),
        pl.BlockSpec((None, k, n), lambda i: (layer, 0, part)),
    ]
    if rope is not None:
        operands += list(rope)
        in_specs += [pl.BlockSpec((tm, LANES), lambda i: (i % tiles_per_seq, 0))] * 2
    return pl.pallas_call(
        functools.partial(_in_proj_kernel, scale=scale, rotary_heads=rotary_heads),
        grid=(m // tm,),
        in_specs=in_specs,
        out_specs=pl.BlockSpec((tm, n), lambda i: (i, 0)),
        out_shape=jax.ShapeDtypeStruct((m, n), _BF16),
        compiler_params=_params("parallel"),
    )(*operands)


def _out_cross_kernel(a_ref, b_ref, c_ref, go_ref, wout_ref, x_ref, gc_ref, wq_ref, kv_ref, wo_ref, o_ref):
    x = x_ref[...]
    off = 0
    for ref in (a_ref, b_ref, c_ref):
        width = ref.shape[1]
        hn = _rms(ref[...], go_ref[:, off:off + width]).astype(_BF16)
        x = x + _dot(hn, wout_ref[off:off + width, :])
        off += width
    q = _dot(_rms(x, gc_ref[...]).astype(_BF16), wq_ref[...]).astype(_BF16)
    kv = kv_ref[0]
    width = CROSS_HEADS * CROSS_HEAD_DIM
    outs = []
    for h in range(CROSS_HEADS):
        sl = slice(h * CROSS_HEAD_DIM, (h + 1) * CROSS_HEAD_DIM)
        s = _dot_nt(q[:, sl], kv[:, sl]) * (CROSS_HEAD_DIM ** -0.5)
        e = jnp.exp(s - jnp.max(s, axis=-1, keepdims=True))
        p = e / jnp.sum(e, axis=-1, keepdims=True)
        outs.append(_dot(p.astype(_BF16), kv[:, width + h * CROSS_HEAD_DIM: width + (h + 1) * CROSS_HEAD_DIM]))
    o_ref[...] = x + _dot(jnp.concatenate(outs, axis=1).astype(_BF16), wo_ref[...])


def _out_cross(parts, g_out, w_out, x, g_cross, w_q, kv, w_o, layer, *, seq, tm):
    m, d = x.shape
    widths = [p.shape[1] for p in parts]
    width = w_q.shape[2]
    mem_len = kv.shape[1]
    assert sum(widths) == w_out.shape[1] and m % tm == 0 and seq % tm == 0
    tiles_per_seq = seq // tm
    return pl.pallas_call(
        _out_cross_kernel,
        grid=(m // tm,),
        in_specs=[pl.BlockSpec((tm, wd), lambda i: (i, 0)) for wd in widths] + [
            pl.BlockSpec((1, w_out.shape[1]), lambda i: (0, 0)),
            pl.BlockSpec((None,) + w_out.shape[1:], lambda i: (layer, 0, 0)),
            pl.BlockSpec((tm, d), lambda i: (i, 0)),
            pl.BlockSpec((1, d), lambda i: (0, 0)),
            pl.BlockSpec((None, d, width), lambda i: (layer, 0, 0)),
            pl.BlockSpec((1, mem_len, 2 * width), lambda i: (i // tiles_per_seq, 0, 0)),
            pl.BlockSpec((None, width, d), lambda i: (layer, 0, 0)),
        ],
        out_specs=pl.BlockSpec((tm, d), lambda i: (i, 0)),
        out_shape=jax.ShapeDtypeStruct((m, d), _F32),
        compiler_params=_params("parallel"),
    )(*parts, g_out.reshape(1, -1), w_out, x, g_cross.reshape(1, d), w_q, kv, w_o)


def _mlp_kernel(*refs, final_norm):
    if final_norm:
        x_ref, g_ref, wu_ref, wd_ref, gf_ref, o_ref, hn_ref = refs
    else:
        x_ref, g_ref, wu_ref, wd_ref, o_ref, hn_ref = refs

    @pl.when(pl.program_id(1) == 0)
    def _start():
        x = x_ref[...]
        hn_ref[...] = _rms(x, g_ref[...]).astype(_BF16)
        o_ref[...] = x

    u = _dot(hn_ref[...], wu_ref[...])
    r = jnp.square(jnp.maximum(u, 0.0)).astype(_BF16)
    o_ref[...] += _dot(r, wd_ref[...])

    if final_norm:
        @pl.when(pl.program_id(1) == pl.num_programs(1) - 1)
        def _finish():
            o_ref[...] = _rms(o_ref[...], gf_ref[...])


def _mlp(x, g, w_up, w_down, layer, *, tm, tf, g_final=None):
    m, d = x.shape
    f = w_up.shape[2]
    assert m % tm == 0 and f % tf == 0
    operands = [x, g.reshape(1, d), w_up, w_down]
    in_specs = [
        pl.BlockSpec((tm, d), lambda i, j: (i, 0)),
        pl.BlockSpec((1, d), lambda i, j: (0, 0)),
        pl.BlockSpec((None, d, tf), lambda i, j: (layer, 0, j)),
        pl.BlockSpec((None, tf, d), lambda i, j: (layer, j, 0)),
    ]
    if g_final is not None:
        operands.append(g_final.reshape(1, d))
        in_specs.append(pl.BlockSpec((1, d), lambda i, j: (0, 0)))
    return pl.pallas_call(
        functools.partial(_mlp_kernel, final_norm=g_final is not None),
        grid=(m // tm, f // tf),
        in_specs=in_specs,
        out_specs=pl.BlockSpec((tm, d), lambda i, j: (i, 0)),
        out_shape=jax.ShapeDtypeStruct((m, d), _F32),
        scratch_shapes=[pltpu.VMEM((tm, d), _BF16)],
        compiler_params=_params("parallel", "arbitrary"),
    )(*operands)


def _moba_kernel(q_ref, k_ref, v_ref, o_ref, km_ref, onehot_ref, *, n_blk):
    qi = pl.program_id(2)
    blk = MOBA_BLOCK
    kb = BLOCKS_PER_STEP * blk
    heads = q_ref.shape[2] // LANES

    @pl.when(qi == 0)
    def _prepare():
        km_ref[...] = jnp.zeros_like(km_ref)
        for h in range(heads):
            for j in range(n_blk):
                rows = k_ref[0, j * blk:(j + 1) * blk, h * LANES:(h + 1) * LANES].astype(_F32)
                km_ref[h, j:j + 1, :] = jnp.sum(rows, axis=0, keepdims=True) * (1.0 / blk)
        key = lax.broadcasted_iota(jnp.int32, onehot_ref.shape, 0)
        lane = lax.broadcasted_iota(jnp.int32, onehot_ref.shape, 1)
        onehot_ref[...] = ((key >= lane * blk) & (key < (lane + 1) * blk)).astype(_BF16)

    row = lax.broadcasted_iota(jnp.int32, (kb, kb), 0)
    col = lax.broadcasted_iota(jnp.int32, (kb, kb), 1)
    nb8 = -(-n_blk // 8) * 8
    blk_id = lax.broadcasted_iota(jnp.int32, (nb8, kb), 0)
    own = qi * BLOCKS_PER_STEP
    for p in range(1, BLOCKS_PER_STEP):
        own = own + (lax.broadcasted_iota(jnp.int32, (nb8, kb), 1) >= p * blk).astype(jnp.int32)

    def masked_queries(h):
        q = q_ref[0, :, h * LANES:(h + 1) * LANES]
        km = km_ref[h]
        km_hi = km.astype(_BF16)
        km_lo = (km - km_hi.astype(_F32)).astype(_BF16)
        gate = (_dot_nt(km_hi, q) + _dot_nt(km_lo, q))[:nb8]
        g = jnp.where(blk_id < own, gate, NEG_INF)
        rank = jnp.zeros((nb8, kb), jnp.int32)
        for j in range(n_blk):
            other = g[j:j + 1, :]
            rank = rank + ((other > g) | ((other == g) & (j < blk_id))).astype(jnp.int32)
        keep = ((blk_id < own) & (rank < MOBA_TOPK)) | (blk_id == own)
        mask_t = jnp.where(keep, 0.0, NEG_INF)
        mask_t = jnp.concatenate([mask_t, jnp.full((LANES - nb8, kb), NEG_INF, _F32)], axis=0)
        return jnp.concatenate([q, mask_t.T.astype(_BF16)], axis=1)

    def key_blocks(h, sbi, q_aug, carry, diagonal):
        hs = slice(h * LANES, (h + 1) * LANES)
        start = pl.multiple_of(sbi * kb, kb)
        k_aug = jnp.concatenate([k_ref[0, pl.ds(start, kb), hs], onehot_ref[pl.ds(start, kb), :]], axis=1)
        s = _dot_nt(q_aug, k_aug)
        if diagonal:
            s = jnp.where(col <= row, s, NEG_INF)
        mx = jnp.max(s, axis=-1, keepdims=True)
        v_aug = jnp.concatenate([v_ref[0, pl.ds(start, kb), hs], ones], axis=1)
        if carry is None:
            return mx, _dot(jnp.exp(s - mx).astype(_BF16), v_aug)
        m, acc = carry
        m_new = jnp.maximum(m, mx)
        return m_new, jnp.exp(m - m_new) * acc + _dot(jnp.exp(s - m_new).astype(_BF16), v_aug)

    ones = jnp.ones((kb, LANES), _BF16)
    q_aug = [masked_queries(h) for h in range(heads)]
    carry0 = tuple(key_blocks(h, qi, q_aug[h], None, True) for h in range(heads))

    def past_blocks(it, carry):
        return tuple(key_blocks(h, qi - 1 - it, q_aug[h], carry[h], False) for h in range(heads))

    carry = lax.fori_loop(0, qi, past_blocks, carry0)
    for h in range(heads):
        acc = carry[h][1]
        o_ref[0, :, h * LANES:(h + 1) * LANES] = acc[:, :LANES] / acc[:, LANES:]


def _head_group_specs(seq, q_rows, head0, hp):
    assert head0 % hp == 0
    w = hp * LANES
    return [
        pl.BlockSpec((1, q_rows, w), lambda b, h, i: (b, i, head0 // hp + h)),
        pl.BlockSpec((1, seq, w), lambda b, h, i: (b, 0, head0 // hp + h)),
        pl.BlockSpec((1, seq, w), lambda b, h, i: (b, 0, head0 // hp + h)),
    ]


def _moba(q, k, v, *, bsz, seq):
    n_blk = seq // MOBA_BLOCK
    hp = N_HEADS_A
    tq = MOBA_BLOCK * BLOCKS_PER_STEP
    assert seq % tq == 0 and MOBA_TOPK < n_blk <= LANES and N_HEADS_A % hp == 0
    return pl.pallas_call(
        functools.partial(_moba_kernel, n_blk=n_blk),
        grid=(bsz, N_HEADS_A // hp, seq // tq),
        in_specs=_head_group_specs(seq, tq, SLAB_A0, hp),
        out_specs=pl.BlockSpec((1, tq, hp * LANES), lambda b, h, i: (b, i, h)),
        out_shape=jax.ShapeDtypeStruct((bsz, seq, N_HEADS_A * HEAD_DIM), _F32),
        scratch_shapes=[pltpu.VMEM((hp, LANES, LANES), _F32), pltpu.VMEM((seq, LANES), _BF16)],
        compiler_params=_params("parallel", "parallel", "arbitrary"),
    )(q, k, v)


def _sb_kernel(q_ref, k_ref, v_ref, o_ref):
    qi = pl.program_id(2)
    blk = SB_BLOCK
    kb = SB_BLOCKS_PER_STEP * blk
    heads = q_ref.shape[2] // LANES
    row2 = lax.broadcasted_iota(jnp.int32, (2 * blk, blk), 0)
    col2 = lax.broadcasted_iota(jnp.int32, (2 * blk, blk), 1)
    later = ((row2 > col2) & (row2 < blk) | (row2 - blk > col2)).astype(_BF16)
    past = lax.broadcasted_iota(jnp.int32, (kb, kb), 1) < lax.broadcasted_iota(jnp.int32, (kb, kb), 0)

    def suffix_sums(x):
        hi = x.astype(_BF16)
        lo = (x - hi.astype(_F32)).astype(_BF16)
        return _dot(jnp.concatenate([hi, lo], axis=1), later)

    def key_blocks(h, sbi, carry, diagonal):
        hs = slice(h * LANES, (h + 1) * LANES)
        start = pl.multiple_of(sbi * kb, kb)
        z = _dot_nt(q_ref[0, :, hs], k_ref[0, pl.ds(start, kb), hs])
        log_beta = jnp.minimum(z, 0.0) - jnp.log(1.0 + jnp.exp(-jnp.abs(z)))
        log_keep = log_beta - z
        if diagonal:
            log_keep = jnp.where(past, log_keep, 0.0)
        sticks = []
        tail = None if carry is None else carry[1]
        for p in reversed(range(SB_BLOCKS_PER_STEP)):
            lk = log_keep[:, p * blk:(p + 1) * blk]
            st = suffix_sums(lk)
            sticks.append(st if tail is None else st + tail)
            total = jnp.sum(lk, axis=-1, keepdims=True)
            tail = total if tail is None else tail + total
        w = jnp.exp(log_beta + jnp.concatenate(sticks[::-1], axis=1))
        if diagonal:
            w = jnp.where(past, w, 0.0)
        out = _dot(w.astype(_BF16), v_ref[0, pl.ds(start, kb), hs])
        return (out if carry is None else carry[0] + out), tail

    def weights_alive(carry):
        worst = jnp.max(functools.reduce(jnp.maximum, [c[1] for c in carry]))
        return (worst >= EXP_UNDERFLOW).astype(jnp.int32)

    carry0 = tuple(key_blocks(h, qi, None, True) for h in range(heads))

    def more_blocks(state):
        it, alive, _ = state
        return (it < qi) & (alive > 0)

    def past_blocks(state):
        it, _, carry = state
        carry = tuple(key_blocks(h, qi - 1 - it, carry[h], False) for h in range(heads))
        return it + 1, weights_alive(carry), carry

    _, _, carry = lax.while_loop(more_blocks, past_blocks, (jnp.int32(0), weights_alive(carry0), carry0))
    for h in range(heads):
        o_ref[0, :, h * LANES:(h + 1) * LANES] = carry[h][0]


def _stick_breaking(q, k, v, *, bsz, seq):
    hp = SB_HEADS_PER_STEP
    tq = SB_BLOCK * SB_BLOCKS_PER_STEP
    assert seq % tq == 0 and N_HEADS_B % hp == 0
    return pl.pallas_call(
        _sb_kernel,
        grid=(bsz, N_HEADS_B // hp, seq // tq),
        in_specs=_head_group_specs(seq, tq, SLAB_B0, hp),
        out_specs=pl.BlockSpec((1, tq, hp * LANES), lambda b, h, i: (b, i, h)),
        out_shape=jax.ShapeDtypeStruct((bsz, seq, N_HEADS_B * HEAD_DIM), _F32),
        compiler_params=_params("parallel", "parallel", "parallel"),
    )(q, k, v)


def _dilated_kernel(q_ref, k_ref, v_ref, o_ref, qf, kf, vf, og, lg, *, seq):
    g = pl.program_id(2)
    qf[...] = q_ref[0].astype(_F32)
    kf[...] = k_ref[0].astype(_F32)
    vf[...] = v_ref[0].astype(_F32)
    row2 = lax.broadcasted_iota(jnp.int32, (BAND, 2 * BAND), 0)
    col2 = lax.broadcasted_iota(jnp.int32, (BAND, 2 * BAND), 1)
    own_block = (col2 >= BAND) & (col2 - BAND <= row2)
    prev_block = (col2 < BAND) & (col2 >= row2)

    def rows(ref, start, d):
        return ref[pl.ds(start, BAND, stride=d), :] if d > 1 else ref[pl.ds(start, BAND), :]

    def unit(gi, d, r, nb):
        q_start = r + nb * (BAND * d)
        p_start = q_start - jnp.where(nb > 0, BAND * d, 0)
        qu = rows(qf, q_start, d).astype(_BF16)
        ku = jnp.concatenate([rows(kf, p_start, d), rows(kf, q_start, d)], axis=0).astype(_BF16)
        vu = jnp.concatenate([rows(vf, p_start, d), rows(vf, q_start, d)], axis=0).astype(_BF16)
        s = jnp.where(own_block | (prev_block & (nb > 0)), _dot_nt(qu, ku), NEG_INF)
        mx = jnp.max(s, axis=-1, keepdims=True)
        e = jnp.exp(s - mx)
        den = jnp.sum(e, axis=-1, keepdims=True)
        out = _dot((e / den).astype(_BF16), vu)
        lse = mx + jnp.log(den)
        dst = pl.ds(q_start, BAND, stride=d) if d > 1 else pl.ds(q_start, BAND)
        og[gi, dst, :] = out
        lg[gi, dst, :] = jnp.broadcast_to(lse, (BAND, LANES))

    def pattern(gi, d):
        def unit_group(i, _):
            for u in range(DILATED_UNITS_PER_STEP):
                idx = i * DILATED_UNITS_PER_STEP + u
                if d == 1:
                    unit(gi, d, 0, idx)
                else:
                    unit(gi, d, lax.rem(idx, d), lax.div(idx, d))
            return 0

        lax.fori_loop(0, seq // (BAND * DILATED_UNITS_PER_STEP), unit_group, 0)

    for gi, d in enumerate(DILATIONS):
        pl.when(g == gi)(functools.partial(pattern, gi, d))

    @pl.when(g == len(DILATIONS) - 1)
    def _mix():
        chunk = 256

        def mix_chunk(c, _):
            sl = pl.ds(pl.multiple_of(c * chunk, chunk), chunk)
            l = [lg[gi, sl, :] for gi in range(len(DILATIONS))]
            mx = jnp.maximum(jnp.maximum(l[0], l[1]), l[2])
            e = [jnp.exp(li - mx) for li in l]
            den = e[0] + e[1] + e[2]
            o_ref[0, sl, :] = (e[0] / den) * og[0, sl, :] + (e[1] / den) * og[1, sl, :] + (e[2] / den) * og[2, sl, :]
            return 0

        lax.fori_loop(0, seq // chunk, mix_chunk, 0)


def _dilated(q, k, v, *, bsz, seq):
    assert seq % (BAND * max(DILATIONS)) == 0 and seq % 256 == 0
    n_pat = len(DILATIONS)
    per = C_HEADS_PER_PATTERN
    spec = pl.BlockSpec((1, seq, LANES), lambda b, j, g: (b, 0, SLAB_C0 + g * per + j))
    return pl.pallas_call(
        functools.partial(_dilated_kernel, seq=seq),
        grid=(bsz, per, n_pat),
        in_specs=[spec, spec, spec],
        out_specs=pl.BlockSpec((1, seq, LANES), lambda b, j, g: (b, 0, j)),
        out_shape=jax.ShapeDtypeStruct((bsz, seq, per * HEAD_DIM), _F32),
        scratch_shapes=[pltpu.VMEM((seq, LANES), _F32)] * 3 + [pltpu.VMEM((n_pat, seq, LANES), _F32)] * 2,
        compiler_params=_params("parallel", "parallel", "arbitrary"),
    )(q, k, v)


def _rope_tables(seq):
    inv_freq = 1.0 / (ROPE_THETA ** (jnp.arange(0, HEAD_DIM, 2, dtype=_F32) / HEAD_DIM))
    ang = jnp.arange(seq, dtype=_F32)[:, None] * inv_freq[None, :]
    cos, sin = jnp.cos(ang), jnp.sin(ang)
    return jnp.concatenate([cos, cos], axis=-1), jnp.concatenate([-sin, sin], axis=-1)


def kernel(x, mem, g_mix, w_in, g_out_a, g_out_b, g_out_c, w_out, g_cross, g_mem,
           w_cq, w_ckv, w_co, g_mlp, w_up, w_down, g_final):
    bsz, seq, d = x.shape
    depth = w_in.shape[0]
    mem_len = mem.shape[1]
    m = bsz * seq
    tm = min(512, seq)
    tm_in = min(1024, seq)
    width = N_MIX_HEADS * HEAD_DIM
    cos2, sin2 = _rope_tables(seq)
    w_in_b, w_out_b, w_cq_b, w_ckv_b, w_co_b, w_up_b, w_down_b = (
        w.astype(_BF16) for w in (w_in, w_out, w_cq, w_ckv, w_co, w_up, w_down))
    g_out = jnp.concatenate([g_out_a, g_out_b, g_out_c], axis=-1)

    xf = x.reshape(m, d)
    memf = mem.reshape(bsz * mem_len, d)
    for l in range(depth):
        q = _in_proj(xf, g_mix[l], w_in_b, l, 0, tm=tm_in, seq=seq, scale=QK_SCALE,
                     rope=(cos2 * QK_SCALE, sin2 * QK_SCALE))
        k = _in_proj(xf, g_mix[l], w_in_b, l, 1, tm=tm_in, seq=seq, rope=(cos2, sin2))
        v = _in_proj(xf, g_mix[l], w_in_b, l, 2, tm=tm_in, seq=seq)
        q, k, v = (a.reshape(bsz, seq, width) for a in (q, k, v))
        parts = [mix(q, k, v, bsz=bsz, seq=seq).reshape(m, -1) for mix in (_moba, _stick_breaking, _dilated)]
        kv = _norm_matmul(memf, g_mem, w_ckv_b, l, tm=min(512, bsz * mem_len), tn=512,
                          out_dtype=_BF16).reshape(bsz, mem_len, -1)
        xf = _out_cross(parts, g_out[l], w_out_b, xf, g_cross[l], w_cq_b, kv, w_co_b, l, seq=seq, tm=tm)
        xf = _mlp(xf, g_mlp[l], w_up_b, w_down_b, l, tm=min(1024, seq), tf=512,
                  g_final=g_final if l == depth - 1 else None)
    return xf.reshape(bsz, seq, d)
```

```python
import functools

import jax
import jax.numpy as jnp
from jax import lax
from jax.experimental import pallas as pl
from jax.experimental.pallas import tpu as pltpu

HEAD_DIM = 128
N_MIX_HEADS = 16
N_HEADS_A = 4
N_HEADS_B = 6
N_HEADS_C = 6
DILATIONS = (1, 4, 16)
BAND = 128
C_HEADS_PER_PATTERN = N_HEADS_C // len(DILATIONS)
MOBA_BLOCK = 256
MOBA_TOPK = 3
SB_BLOCK = 256
BLOCKS_PER_STEP = 2
SB_BLOCKS_PER_STEP = 2
SB_HEADS_PER_STEP = 3
DILATED_UNITS_PER_STEP = 16
ROPE_THETA = 10000.0
CROSS_HEADS = 4
CROSS_HEAD_DIM = 128
RMS_EPS = 1e-6
NEG_INF = -1e30
EXP2_UNDERFLOW = -150.0
LOG2_E = 1.4426950408889634
F32_SIGN_BIT = 0x80000000
QK_SCALE = HEAD_DIM ** -0.5 * LOG2_E

LANES = 128
V7X_VMEM_LIMIT_BYTES = 56 * 1024 * 1024

_BF16 = jnp.bfloat16
_F32 = jnp.float32

HEAD_A0, HEAD_B0, HEAD_C0 = 0, N_HEADS_A, N_HEADS_A + N_HEADS_B
SLAB_B0, SLAB_C0, SLAB_A0 = 0, N_HEADS_B, N_HEADS_B + N_HEADS_C


def _slab_of_head(h):
    if h < HEAD_B0:
        return SLAB_A0 + h - HEAD_A0
    return SLAB_B0 + h - HEAD_B0 if h < HEAD_C0 else SLAB_C0 + h - HEAD_C0


def _dot(a, b):
    return jnp.dot(a, b, preferred_element_type=_F32)


def _dot_nt(a, b):
    return lax.dot_general(a, b, (((1,), (1,)), ((), ())), preferred_element_type=_F32)


def _rms(x, g):
    ms = jnp.mean(x * x, axis=-1, keepdims=True)
    return x * lax.rsqrt(ms + RMS_EPS) * g


def _params(*semantics):
    return pltpu.CompilerParams(dimension_semantics=semantics, vmem_limit_bytes=V7X_VMEM_LIMIT_BYTES)


def _norm_matmul_kernel(x_ref, g_ref, w_ref, o_ref, hn_ref):
    @pl.when(pl.program_id(1) == 0)
    def _normalise():
        hn_ref[...] = _rms(x_ref[...], g_ref[...]).astype(_BF16)

    o_ref[...] = _dot(hn_ref[...], w_ref[...]).astype(o_ref.dtype)


def _norm_matmul(x, g, w, layer, *, tm, tn, out_dtype):
    m, k = x.shape
    n = w.shape[2]
    assert m % tm == 0 and n % tn == 0
    return pl.pallas_call(
        _norm_matmul_kernel,
        grid=(m // tm, n // tn),
        in_specs=[
            pl.BlockSpec((tm, k), lambda i, j: (i, 0)),
            pl.BlockSpec((1, k), lambda i, j: (0, 0)),
            pl.BlockSpec((None, k, tn), lambda i, j: (layer, 0, j)),
        ],
        out_specs=pl.BlockSpec((tm, tn), lambda i, j: (i, j)),
        out_shape=jax.ShapeDtypeStruct((m, n), out_dtype),
        scratch_shapes=[pltpu.VMEM((tm, k), _BF16)],
        compiler_params=_params("parallel", "arbitrary"),
    )(x, g.reshape(1, k), w)


def _in_proj_kernel(*refs, scale, rotary_heads):
    if any(rotary_heads):
        x_ref, g_ref, w_ref, cos_ref, sin_ref, o_ref = refs
        c, s = cos_ref[...], sin_ref[...]
    else:
        x_ref, g_ref, w_ref, o_ref = refs
    acc = _dot(_rms(x_ref[...], g_ref[...]).astype(_BF16), w_ref[...])
    for h, rotary in enumerate(rotary_heads):
        a = acc[:, h * LANES:(h + 1) * LANES]
        if rotary:
            a = a * c + pltpu.roll(a, LANES // 2, 1) * s
        elif scale != 1.0:
            a = a * scale
        dst = _slab_of_head(h)
        o_ref[:, dst * LANES:(dst + 1) * LANES] = a.astype(o_ref.dtype)


def _in_proj(x, g, w, layer, part, *, tm, seq, scale=1.0, rope=None):
    m, k = x.shape
    n = N_MIX_HEADS * HEAD_DIM
    assert m % tm == 0 and seq % tm == 0 and w.shape[2] == 3 * n
    rotary_heads = tuple(rope is not None and not HEAD_B0 <= h < HEAD_C0 for h in range(N_MIX_HEADS))
    tiles_per_seq = seq // tm
    operands = [x, g.reshape(1, k), w]
    in_specs = [
        pl.BlockSpec((tm, k), lambda i: (i, 0)),
        pl.BlockSpec((1, k), lambda i: (0, 0)),
        pl.BlockSpec((None, k, n), lambda i: (layer, 0, part)),
    ]
    if rope is not None:
        operands += list(rope)
        in_specs += [pl.BlockSpec((tm, LANES), lambda i: (i % tiles_per_seq, 0))] * 2
    return pl.pallas_call(
        functools.partial(_in_proj_kernel, scale=scale, rotary_heads=rotary_heads),
        grid=(m // tm,),
        in_specs=in_specs,
        out_specs=pl.BlockSpec((tm, n), lambda i: (i, 0)),
        out_shape=jax.ShapeDtypeStruct((m, n), _BF16),
        compiler_params=_params("parallel"),
    )(*operands)


def _out_cross_kernel(a_ref, b_ref, c_ref, go_ref, wout_ref, x_ref, gc_ref, wq_ref, kv_ref, wo_ref, o_ref):
    x = x_ref[...]
    off = 0
    for ref in (a_ref, b_ref, c_ref):
        width = ref.shape[1]
        hn = _rms(ref[...], go_ref[:, off:off + width]).astype(_BF16)
        x = x + _dot(hn, wout_ref[off:off + width, :])
        off += width
    q = _dot(_rms(x, gc_ref[...]).astype(_BF16), wq_ref[...]).astype(_BF16)
    kv = kv_ref[0]
    width = CROSS_HEADS * CROSS_HEAD_DIM
    outs = []
    for h in range(CROSS_HEADS):
        sl = slice(h * CROSS_HEAD_DIM, (h + 1) * CROSS_HEAD_DIM)
        s = _dot_nt(q[:, sl], kv[:, sl]) * (CROSS_HEAD_DIM ** -0.5)
        e = jnp.exp(s - jnp.max(s, axis=-1, keepdims=True))
        p = e / jnp.sum(e, axis=-1, keepdims=True)
        outs.append(_dot(p.astype(_BF16), kv[:, width + h * CROSS_HEAD_DIM: width + (h + 1) * CROSS_HEAD_DIM]))
    o_ref[...] = x + _dot(jnp.concatenate(outs, axis=1).astype(_BF16), wo_ref[...])


def _out_cross(parts, g_out, w_out, x, g_cross, w_q, kv, w_o, layer, *, seq, tm):
    m, d = x.shape
    widths = [p.shape[1] for p in parts]
    width = w_q.shape[2]
    mem_len = kv.shape[1]
    assert sum(widths) == w_out.shape[1] and m % tm == 0 and seq % tm == 0
    tiles_per_seq = seq // tm
    return pl.pallas_call(
        _out_cross_kernel,
        grid=(m // tm,),
        in_specs=[pl.BlockSpec((tm, wd), lambda i: (i, 0)) for wd in widths] + [
            pl.BlockSpec((1, w_out.shape[1]), lambda i: (0, 0)),
            pl.BlockSpec((None,) + w_out.shape[1:], lambda i: (layer, 0, 0)),
            pl.BlockSpec((tm, d), lambda i: (i, 0)),
            pl.BlockSpec((1, d), lambda i: (0, 0)),
            pl.BlockSpec((None, d, width), lambda i: (layer, 0, 0)),
            pl.BlockSpec((1, mem_len, 2 * width), lambda i: (i // tiles_per_seq, 0, 0)),
            pl.BlockSpec((None, width, d), lambda i: (layer, 0, 0)),
        ],
        out_specs=pl.BlockSpec((tm, d), lambda i: (i, 0)),
        out_shape=jax.ShapeDtypeStruct((m, d), _F32),
        compiler_params=_params("parallel"),
    )(*parts, g_out.reshape(1, -1), w_out, x, g_cross.reshape(1, d), w_q, kv, w_o)


def _mlp_kernel(*refs, final_norm):
    if final_norm:
        x_ref, g_ref, wu_ref, wd_ref, gf_ref, o_ref, hn_ref = refs
    else:
        x_ref, g_ref, wu_ref, wd_ref, o_ref, hn_ref = refs

    @pl.when(pl.program_id(1) == 0)
    def _start():
        x = x_ref[...]
        hn_ref[...] = _rms(x, g_ref[...]).astype(_BF16)
        o_ref[...] = x

    u = _dot(hn_ref[...], wu_ref[...])
    r = jnp.square(jnp.maximum(u, 0.0)).astype(_BF16)
    o_ref[...] += _dot(r, wd_ref[...])

    if final_norm:
        @pl.when(pl.program_id(1) == pl.num_programs(1) - 1)
        def _finish():
            o_ref[...] = _rms(o_ref[...], gf_ref[...])


def _mlp(x, g, w_up, w_down, layer, *, tm, tf, g_final=None):
    m, d = x.shape
    f = w_up.shape[2]
    assert m % tm == 0 and f % tf == 0
    operands = [x, g.reshape(1, d), w_up, w_down]
    in_specs = [
        pl.BlockSpec((tm, d), lambda i, j: (i, 0)),
        pl.BlockSpec((1, d), lambda i, j: (0, 0)),
        pl.BlockSpec((None, d, tf), lambda i, j: (layer, 0, j)),
        pl.BlockSpec((None, tf, d), lambda i, j: (layer, j, 0)),
    ]
    if g_final is not None:
        operands.append(g_final.reshape(1, d))
        in_specs.append(pl.BlockSpec((1, d), lambda i, j: (0, 0)))
    return pl.pallas_call(
        functools.partial(_mlp_kernel, final_norm=g_final is not None),
        grid=(m // tm, f // tf),
        in_specs=in_specs,
        out_specs=pl.BlockSpec((tm, d), lambda i, j: (i, 0)),
        out_shape=jax.ShapeDtypeStruct((m, d), _F32),
        scratch_shapes=[pltpu.VMEM((tm, d), _BF16)],
        compiler_params=_params("parallel", "arbitrary"),
    )(*operands)


def _moba_kernel(q_ref, k_ref, v_ref, o_ref, km_ref, onehot_ref, *, n_blk):
    qi = pl.program_id(2)
    blk = MOBA_BLOCK
    kb = BLOCKS_PER_STEP * blk
    heads = q_ref.shape[2] // LANES

    @pl.when(qi == 0)
    def _prepare():
        km_ref[...] = jnp.zeros_like(km_ref)
        for h in range(heads):
            for j in range(n_blk):
                rows = k_ref[0, j * blk:(j + 1) * blk, h * LANES:(h + 1) * LANES].astype(_F32)
                km_ref[h, j:j + 1, :] = jnp.sum(rows, axis=0, keepdims=True) * (1.0 / blk)
        key = lax.broadcasted_iota(jnp.int32, onehot_ref.shape, 0)
        lane = lax.broadcasted_iota(jnp.int32, onehot_ref.shape, 1)
        onehot_ref[...] = ((key >= lane * blk) & (key < (lane + 1) * blk)).astype(_BF16)

    row = lax.broadcasted_iota(jnp.int32, (kb, kb), 0)
    col = lax.broadcasted_iota(jnp.int32, (kb, kb), 1)
    nb8 = -(-n_blk // 8) * 8
    blk_id = lax.broadcasted_iota(jnp.int32, (nb8, kb), 0)
    own = qi * BLOCKS_PER_STEP
    for p in range(1, BLOCKS_PER_STEP):
        own = own + (lax.broadcasted_iota(jnp.int32, (nb8, kb), 1) >= p * blk).astype(jnp.int32)

    def masked_queries(h):
        q = q_ref[0, :, h * LANES:(h + 1) * LANES]
        km = km_ref[h]
        km_hi = km.astype(_BF16)
        km_lo = (km - km_hi.astype(_F32)).astype(_BF16)
        gate = (_dot_nt(km_hi, q) + _dot_nt(km_lo, q))[:nb8]
        g = jnp.where(blk_id < own, gate, NEG_INF)
        rank = jnp.zeros((nb8, kb), jnp.int32)
        for j in range(n_blk):
            other = g[j:j + 1, :]
            rank = rank + ((other > g) | ((other == g) & (j < blk_id))).astype(jnp.int32)
        keep = ((blk_id < own) & (rank < MOBA_TOPK)) | (blk_id == own)
        mask_t = jnp.where(keep, 0.0, NEG_INF)
        mask_t = jnp.concatenate([mask_t, jnp.full((LANES - nb8, kb), NEG_INF, _F32)], axis=0)
        return jnp.concatenate([q, mask_t.T.astype(_BF16)], axis=1)

    def key_blocks(h, sbi, q_aug, carry, diagonal):
        hs = slice(h * LANES, (h + 1) * LANES)
        start = pl.multiple_of(sbi * kb, kb)
        k_aug = jnp.concatenate([k_ref[0, pl.ds(start, kb), hs], onehot_ref[pl.ds(start, kb), :]], axis=1)
        s = _dot_nt(q_aug, k_aug)
        if diagonal:
            s = jnp.where(col <= row, s, NEG_INF)
        mx = jnp.max(s, axis=-1, keepdims=True)
        v_aug = jnp.concatenate([v_ref[0, pl.ds(start, kb), hs], ones], axis=1)
        if carry is None:
            return mx, _dot(jnp.exp2(s - mx).astype(_BF16), v_aug)
        m, acc = carry
        m_new = jnp.maximum(m, mx)
        return m_new, jnp.exp2(m - m_new) * acc + _dot(jnp.exp2(s - m_new).astype(_BF16), v_aug)

    ones = jnp.ones((kb, LANES), _BF16)
    q_aug = [masked_queries(h) for h in range(heads)]
    carry0 = tuple(key_blocks(h, qi, q_aug[h], None, True) for h in range(heads))

    def past_blocks(it, carry):
        return tuple(key_blocks(h, qi - 1 - it, q_aug[h], carry[h], False) for h in range(heads))

    carry = lax.fori_loop(0, qi, past_blocks, carry0)
    for h in range(heads):
        acc = carry[h][1]
        o_ref[0, :, h * LANES:(h + 1) * LANES] = acc[:, :LANES] / acc[:, LANES:]


def _head_group_specs(seq, q_rows, head0, hp):
    assert head0 % hp == 0
    w = hp * LANES
    return [
        pl.BlockSpec((1, q_rows, w), lambda b, h, i: (b, i, head0 // hp + h)),
        pl.BlockSpec((1, seq, w), lambda b, h, i: (b, 0, head0 // hp + h)),
        pl.BlockSpec((1, seq, w), lambda b, h, i: (b, 0, head0 // hp + h)),
    ]


def _moba(q, k, v, *, bsz, seq):
    n_blk = seq // MOBA_BLOCK
    hp = N_HEADS_A
    tq = MOBA_BLOCK * BLOCKS_PER_STEP
    assert seq % tq == 0 and MOBA_TOPK < n_blk <= LANES and N_HEADS_A % hp == 0
    return pl.pallas_call(
        functools.partial(_moba_kernel, n_blk=n_blk),
        grid=(bsz, N_HEADS_A // hp, seq // tq),
        in_specs=_head_group_specs(seq, tq, SLAB_A0, hp),
        out_specs=pl.BlockSpec((1, tq, hp * LANES), lambda b, h, i: (b, i, h)),
        out_shape=jax.ShapeDtypeStruct((bsz, seq, N_HEADS_A * HEAD_DIM), _F32),
        scratch_shapes=[pltpu.VMEM((hp, LANES, LANES), _F32), pltpu.VMEM((seq, LANES), _BF16)],
        compiler_params=_params("parallel", "parallel", "arbitrary"),
    )(q, k, v)


def _sb_kernel(q_ref, k_ref, v_ref, o_ref):
    qi = pl.program_id(2)
    blk = SB_BLOCK
    kb = SB_BLOCKS_PER_STEP * blk
    heads = q_ref.shape[2] // LANES
    row2 = lax.broadcasted_iota(jnp.int32, (2 * blk, blk), 0)
    col2 = lax.broadcasted_iota(jnp.int32, (2 * blk, blk), 1)
    later = ((row2 > col2) & (row2 < blk) | (row2 - blk > col2)).astype(_BF16)
    past = lax.broadcasted_iota(jnp.int32, (kb, kb), 1) < lax.broadcasted_iota(jnp.int32, (kb, kb), 0)

    def suffix_sums(x):
        hi = x.astype(_BF16)
        lo = (x - hi.astype(_F32)).astype(_BF16)
        return _dot(jnp.concatenate([hi, lo], axis=1), later)

    def key_blocks(h, sbi, carry, diagonal):
        hs = slice(h * LANES, (h + 1) * LANES)
        start = pl.multiple_of(sbi * kb, kb)
        z = _dot_nt(q_ref[0, :, hs], k_ref[0, pl.ds(start, kb), hs])
        neg_abs = lax.bitcast_convert_type(lax.bitcast_convert_type(z, jnp.uint32) | jnp.uint32(F32_SIGN_BIT), _F32)
        log_beta = jnp.minimum(z, 0.0) - jnp.log(1.0 + jnp.exp2(neg_abs)) * LOG2_E
        log_keep = log_beta - z
        if diagonal:
            log_keep = jnp.where(past, log_keep, 0.0)
        sticks = []
        tail = None if carry is None else carry[1]
        for p in reversed(range(SB_BLOCKS_PER_STEP)):
            lk = log_keep[:, p * blk:(p + 1) * blk]
            st = suffix_sums(lk)
            sticks.append(st if tail is None else st + tail)
            total = jnp.sum(lk, axis=-1, keepdims=True)
            tail = total if tail is None else tail + total
        w = jnp.exp2(log_beta + jnp.concatenate(sticks[::-1], axis=1))
        if diagonal:
            w = jnp.where(past, w, 0.0)
        out = _dot(w.astype(_BF16), v_ref[0, pl.ds(start, kb), hs])
        return (out if carry is None else carry[0] + out), tail

    def weights_alive(carry):
        worst = jnp.max(functools.reduce(jnp.maximum, [c[1] for c in carry]))
        return (worst >= EXP2_UNDERFLOW).astype(jnp.int32)

    carry0 = tuple(key_blocks(h, qi, None, True) for h in range(heads))

    def more_blocks(state):
        it, alive, _ = state
        return (it < qi) & (alive > 0)

    def past_blocks(state):
        it, _, carry = state
        carry = tuple(key_blocks(h, qi - 1 - it, carry[h], False) for h in range(heads))
        return it + 1, weights_alive(carry), carry

    _, _, carry = lax.while_loop(more_blocks, past_blocks, (jnp.int32(0), weights_alive(carry0), carry0))
    for h in range(heads):
        o_ref[0, :, h * LANES:(h + 1) * LANES] = carry[h][0]


def _stick_breaking(q, k, v, *, bsz, seq):
    hp = SB_HEADS_PER_STEP
    tq = SB_BLOCK * SB_BLOCKS_PER_STEP
    assert seq % tq == 0 and N_HEADS_B % hp == 0
    return pl.pallas_call(
        _sb_kernel,
        grid=(bsz, N_HEADS_B // hp, seq // tq),
        in_specs=_head_group_specs(seq, tq, SLAB_B0, hp),
        out_specs=pl.BlockSpec((1, tq, hp * LANES), lambda b, h, i: (b, i, h)),
        out_shape=jax.ShapeDtypeStruct((bsz, seq, N_HEADS_B * HEAD_DIM), _F32),
        compiler_params=_params("parallel", "parallel", "parallel"),
    )(q, k, v)


def _dilated_kernel(q_ref, k_ref, v_ref, o_ref, qf, kf, vf, og, lg, *, seq):
    g = pl.program_id(2)
    qf[...] = q_ref[0].astype(_F32)
    kf[...] = k_ref[0].astype(_F32)
    vf[...] = v_ref[0].astype(_F32)
    row2 = lax.broadcasted_iota(jnp.int32, (BAND, 2 * BAND), 0)
    col2 = lax.broadcasted_iota(jnp.int32, (BAND, 2 * BAND), 1)
    own_block = (col2 >= BAND) & (col2 - BAND <= row2)
    prev_block = (col2 < BAND) & (col2 >= row2)

    def rows(ref, start, d):
        return ref[pl.ds(start, BAND, stride=d), :] if d > 1 else ref[pl.ds(start, BAND), :]

    def unit(gi, d, r, nb):
        q_start = r + nb * (BAND * d)
        p_start = q_start - jnp.where(nb > 0, BAND * d, 0)
        qu = rows(qf, q_start, d).astype(_BF16)
        ku = jnp.concatenate([rows(kf, p_start, d), rows(kf, q_start, d)], axis=0).astype(_BF16)
        vu = jnp.concatenate([rows(vf, p_start, d), rows(vf, q_start, d)], axis=0).astype(_BF16)
        s = jnp.where(own_block | (prev_block & (nb > 0)), _dot_nt(qu, ku), NEG_INF)
        mx = jnp.max(s, axis=-1, keepdims=True)
        e = jnp.exp2(s - mx)
        den = jnp.sum(e, axis=-1, keepdims=True)
        out = _dot((e / den).astype(_BF16), vu)
        lse = mx + jnp.log(den) * LOG2_E
        dst = pl.ds(q_start, BAND, stride=d) if d > 1 else pl.ds(q_start, BAND)
        og[gi, dst, :] = out
        lg[gi, dst, :] = jnp.broadcast_to(lse, (BAND, LANES))

    def pattern(gi, d):
        def unit_group(i, _):
            for u in range(DILATED_UNITS_PER_STEP):
                idx = i * DILATED_UNITS_PER_STEP + u
                if d == 1:
                    unit(gi, d, 0, idx)
                else:
                    unit(gi, d, lax.rem(idx, d), lax.div(idx, d))
            return 0

        lax.fori_loop(0, seq // (BAND * DILATED_UNITS_PER_STEP), unit_group, 0)

    for gi, d in enumerate(DILATIONS):
        pl.when(g == gi)(functools.partial(pattern, gi, d))

    @pl.when(g == len(DILATIONS) - 1)
    def _mix():
        chunk = 256

        def mix_chunk(c, _):
            sl = pl.ds(pl.multiple_of(c * chunk, chunk), chunk)
            l = [lg[gi, sl, :] for gi in range(len(DILATIONS))]
            mx = jnp.maximum(jnp.maximum(l[0], l[1]), l[2])
            e = [jnp.exp2(li - mx) for li in l]
            den = e[0] + e[1] + e[2]
            o_ref[0, sl, :] = (e[0] / den) * og[0, sl, :] + (e[1] / den) * og[1, sl, :] + (e[2] / den) * og[2, sl, :]
            return 0

        lax.fori_loop(0, seq // chunk, mix_chunk, 0)


def _dilated(q, k, v, *, bsz, seq):
    assert seq % (BAND * max(DILATIONS)) == 0 and seq % 256 == 0
    n_pat = len(DILATIONS)
    per = C_HEADS_PER_PATTERN
    spec = pl.BlockSpec((1, seq, LANES), lambda b, j, g: (b, 0, SLAB_C0 + g * per + j))
    return pl.pallas_call(
        functools.partial(_dilated_kernel, seq=seq),
        grid=(bsz, per, n_pat),
        in_specs=[spec, spec, spec],
        out_specs=pl.BlockSpec((1, seq, LANES), lambda b, j, g: (b, 0, j)),
        out_shape=jax.ShapeDtypeStruct((bsz, seq, per * HEAD_DIM), _F32),
        scratch_shapes=[pltpu.VMEM((seq, LANES), _F32)] * 3 + [pltpu.VMEM((n_pat, seq, LANES), _F32)] * 2,
        compiler_params=_params("parallel", "parallel", "arbitrary"),
    )(q, k, v)


def _rope_tables(seq):
    inv_freq = 1.0 / (ROPE_THETA ** (jnp.arange(0, HEAD_DIM, 2, dtype=_F32) / HEAD_DIM))
    ang = jnp.arange(seq, dtype=_F32)[:, None] * inv_freq[None, :]
    cos, sin = jnp.cos(ang), jnp.sin(ang)
    return jnp.concatenate([cos, cos], axis=-1), jnp.concatenate([-sin, sin], axis=-1)


def kernel(x, mem, g_mix, w_in, g_out_a, g_out_b, g_out_c, w_out, g_cross, g_mem,
           w_cq, w_ckv, w_co, g_mlp, w_up, w_down, g_final):
    bsz, seq, d = x.shape
    depth = w_in.shape[0]
    mem_len = mem.shape[1]
    m = bsz * seq
    tm = min(512, seq)
    tm_in = min(1024, seq)
    width = N_MIX_HEADS * HEAD_DIM
    cos2, sin2 = _rope_tables(seq)
    w_in_b, w_out_b, w_cq_b, w_ckv_b, w_co_b, w_up_b, w_down_b = (
        w.astype(_BF16) for w in (w_in, w_out, w_cq, w_ckv, w_co, w_up, w_down))
    g_out = jnp.concatenate([g_out_a, g_out_b, g_out_c], axis=-1)

    xf = x.reshape(m, d)
    memf = mem.reshape(bsz * mem_len, d)
    for l in range(depth):
        q = _in_proj(xf, g_mix[l], w_in_b, l, 0, tm=tm_in, seq=seq, scale=QK_SCALE,
                     rope=(cos2 * QK_SCALE, sin2 * QK_SCALE))
        k = _in_proj(xf, g_mix[l], w_in_b, l, 1, tm=tm_in, seq=seq, rope=(cos2, sin2))
        v = _in_proj(xf, g_mix[l], w_in_b, l, 2, tm=tm_in, seq=seq)
        q, k, v = (a.reshape(bsz, seq, width) for a in (q, k, v))
        parts = [mix(q, k, v, bsz=bsz, seq=seq).reshape(m, -1) for mix in (_moba, _stick_breaking, _dilated)]
        kv = _norm_matmul(memf, g_mem, w_ckv_b, l, tm=min(512, bsz * mem_len), tn=512,
                          out_dtype=_BF16).reshape(bsz, mem_len, -1)
        xf = _out_cross(parts, g_out[l], w_out_b, xf, g_cross[l], w_cq_b, kv, w_co_b, l, seq=seq, tm=tm)
        xf = _mlp(xf, g_mlp[l], w_up_b, w_down_b, l, tm=min(1024, seq), tf=512,
                  g_final=g_final if l == depth - 1 else None)
    return xf.reshape(bsz, seq, d)
```

```python
import functools

import jax
import jax.numpy as jnp
from jax import lax
from jax.experimental import pallas as pl
from jax.experimental.pallas import tpu as pltpu

HEAD_DIM = 128
N_MIX_HEADS = 16
N_HEADS_A = 4
N_HEADS_B = 6
N_HEADS_C = 6
DILATIONS = (1, 4, 16)
BAND = 128
C_HEADS_PER_PATTERN = N_HEADS_C // len(DILATIONS)
MOBA_BLOCK = 256
MOBA_TOPK = 3
SB_BLOCK = 256
BLOCKS_PER_STEP = 2
MOBA_Q_BLOCKS = 2
MOBA_HEADS_PER_STEP = 4
SB_BLOCKS_PER_STEP = 2
SB_HEADS_PER_STEP = 6
DILATED_UNITS_PER_STEP = 32
ROPE_THETA = 10000.0
CROSS_HEADS = 4
CROSS_HEAD_DIM = 128
RMS_EPS = 1e-6
NEG_INF = -1e30
EXP2_UNDERFLOW = -150.0
LOG2_E = 1.4426950408889634
F32_SIGN_BIT = 0x80000000
QK_SCALE = HEAD_DIM ** -0.5 * LOG2_E

LANES = 128
V7X_VMEM_BYTES = 64 * 1024 * 1024
V7X_VMEM_LIMIT_BYTES = V7X_VMEM_BYTES * 7 // 8

IN_PROJ_ROWS = 1024
OUT_CROSS_ROWS = 512
MLP_ROWS = 1024
MLP_HIDDEN_TILE = 512
MEM_KV_ROWS = 512
MEM_KV_COLS = 512
MIX_ROWS = 256

_BF16 = jnp.bfloat16
_F32 = jnp.float32

HEAD_A0, HEAD_B0, HEAD_C0 = 0, N_HEADS_A, N_HEADS_A + N_HEADS_B
SLAB_B0, SLAB_C0, SLAB_A0 = 0, N_HEADS_B, N_HEADS_B + N_HEADS_C


def _slab_of_head(h):
    if h < HEAD_B0:
        return SLAB_A0 + h - HEAD_A0
    return SLAB_B0 + h - HEAD_B0 if h < HEAD_C0 else SLAB_C0 + h - HEAD_C0


def _dot(a, b):
    return jnp.dot(a, b, preferred_element_type=_F32)


def _dot_nt(a, b):
    return lax.dot_general(a, b, (((1,), (1,)), ((), ())), preferred_element_type=_F32)


def _rms(x, g):
    ms = jnp.mean(x * x, axis=-1, keepdims=True)
    return x * lax.rsqrt(ms + RMS_EPS) * g


def _params(*semantics):
    return pltpu.CompilerParams(dimension_semantics=semantics, vmem_limit_bytes=V7X_VMEM_LIMIT_BYTES)


def _norm_matmul_kernel(x_ref, g_ref, w_ref, o_ref, hn_ref):
    @pl.when(pl.program_id(1) == 0)
    def _normalise():
        hn_ref[...] = _rms(x_ref[...], g_ref[...]).astype(_BF16)

    o_ref[...] = _dot(hn_ref[...], w_ref[...]).astype(o_ref.dtype)


def _norm_matmul(x, g, w, layer, *, tm, tn, out_dtype):
    m, k = x.shape
    n = w.shape[2]
    assert m % tm == 0 and n % tn == 0
    return pl.pallas_call(
        _norm_matmul_kernel,
        grid=(m // tm, n // tn),
        in_specs=[
            pl.BlockSpec((tm, k), lambda i, j: (i, 0)),
            pl.BlockSpec((1, k), lambda i, j: (0, 0)),
            pl.BlockSpec((None, k, tn), lambda i, j: (layer, 0, j)),
        ],
        out_specs=pl.BlockSpec((tm, tn), lambda i, j: (i, j)),
        out_shape=jax.ShapeDtypeStruct((m, n), out_dtype),
        scratch_shapes=[pltpu.VMEM((tm, k), _BF16)],
        compiler_params=_params("parallel", "arbitrary"),
    )(x, g.reshape(1, k), w)


def _in_proj_kernel(*refs, scale, rotary_heads):
    if any(rotary_heads):
        x_ref, g_ref, w_ref, cos_ref, sin_ref, o_ref = refs
        c, s = cos_ref[...], sin_ref[...]
    else:
        x_ref, g_ref, w_ref, o_ref = refs
    acc = _dot(_rms(x_ref[...], g_ref[...]).astype(_BF16), w_ref[...])
    for h, rotary in enumerate(rotary_heads):
        a = acc[:, h * LANES:(h + 1) * LANES]
        if rotary:
            a = a * c + pltpu.roll(a, LANES // 2, 1) * s
        elif scale != 1.0:
            a = a * scale
        dst = _slab_of_head(h)
        o_ref[:, dst * LANES:(dst + 1) * LANES] = a.astype(o_ref.dtype)


def _in_proj(x, g, w, layer, part, *, tm, seq, scale=1.0, rope=None):
    m, k = x.shape
    n = N_MIX_HEADS * HEAD_DIM
    assert m % tm == 0 and seq % tm == 0 and w.shape[2] == 3 * n
    rotary_heads = tuple(rope is not None and not HEAD_B0 <= h < HEAD_C0 for h in range(N_MIX_HEADS))
    tiles_per_seq = seq // tm
    operands = [x, g.reshape(1, k), w]
    in_specs = [
        pl.BlockSpec((tm, k), lambda i: (i, 0)),
        pl.BlockSpec((1, k), lambda i: (0, 0)),
        pl.BlockSpec((None, k, n), lambda i: (layer, 0, part)),
    ]
    if rope is not None:
        operands += list(rope)
        in_specs += [pl.BlockSpec((tm, LANES), lambda i: (i % tiles_per_seq, 0))] * 2
    return pl.pallas_call(
        functools.partial(_in_proj_kernel, scale=scale, rotary_heads=rotary_heads),
        grid=(m // tm,),
        in_specs=in_specs,
        out_specs=pl.BlockSpec((tm, n), lambda i: (i, 0)),
        out_shape=jax.ShapeDtypeStruct((m, n), _BF16),
        compiler_params=_params("parallel"),
    )(*operands)


def _out_cross_kernel(a_ref, b_ref, c_ref, go_ref, wout_ref, x_ref, gc_ref, wq_ref, kv_ref, wo_ref, o_ref):
    x = x_ref[...]
    off = 0
    for ref in (a_ref, b_ref, c_ref):
        width = ref.shape[1]
        hn = _rms(ref[...], go_ref[:, off:off + width]).astype(_BF16)
        x = x + _dot(hn, wout_ref[off:off + width, :])
        off += width
    q = _dot(_rms(x, gc_ref[...]).astype(_BF16), wq_ref[...]).astype(_BF16)
    kv = kv_ref[0]
    width = CROSS_HEADS * CROSS_HEAD_DIM
    outs = []
    for h in range(CROSS_HEADS):
        sl = slice(h * CROSS_HEAD_DIM, (h + 1) * CROSS_HEAD_DIM)
        s = _dot_nt(q[:, sl], kv[:, sl]) * (CROSS_HEAD_DIM ** -0.5)
        e = jnp.exp(s - jnp.max(s, axis=-1, keepdims=True))
        p = e / jnp.sum(e, axis=-1, keepdims=True)
        outs.append(_dot(p.astype(_BF16), kv[:, width + h * CROSS_HEAD_DIM: width + (h + 1) * CROSS_HEAD_DIM]))
    o_ref[...] = x + _dot(jnp.concatenate(outs, axis=1).astype(_BF16), wo_ref[...])


def _out_cross(parts, g_out, w_out, x, g_cross, w_q, kv, w_o, layer, *, seq, tm):
    m, d = x.shape
    widths = [p.shape[1] for p in parts]
    width = w_q.shape[2]
    mem_len = kv.shape[1]
    assert sum(widths) == w_out.shape[1] and m % tm == 0 and seq % tm == 0
    tiles_per_seq = seq // tm
    return pl.pallas_call(
        _out_cross_kernel,
        grid=(m // tm,),
        in_specs=[pl.BlockSpec((tm, wd), lambda i: (i, 0)) for wd in widths] + [
            pl.BlockSpec((1, w_out.shape[1]), lambda i: (0, 0)),
            pl.BlockSpec((None,) + w_out.shape[1:], lambda i: (layer, 0, 0)),
            pl.BlockSpec((tm, d), lambda i: (i, 0)),
            pl.BlockSpec((1, d), lambda i: (0, 0)),
            pl.BlockSpec((None, d, width), lambda i: (layer, 0, 0)),
            pl.BlockSpec((1, mem_len, 2 * width), lambda i: (i // tiles_per_seq, 0, 0)),
            pl.BlockSpec((None, width, d), lambda i: (layer, 0, 0)),
        ],
        out_specs=pl.BlockSpec((tm, d), lambda i: (i, 0)),
        out_shape=jax.ShapeDtypeStruct((m, d), _F32),
        compiler_params=_params("parallel"),
    )(*parts, g_out.reshape(1, -1), w_out, x, g_cross.reshape(1, d), w_q, kv, w_o)


def _mlp_kernel(*refs, final_norm):
    if final_norm:
        x_ref, g_ref, wu_ref, wd_ref, gf_ref, o_ref, hn_ref = refs
    else:
        x_ref, g_ref, wu_ref, wd_ref, o_ref, hn_ref = refs

    @pl.when(pl.program_id(1) == 0)
    def _start():
        x = x_ref[...]
        hn_ref[...] = _rms(x, g_ref[...]).astype(_BF16)
        o_ref[...] = x

    u = _dot(hn_ref[...], wu_ref[...])
    r = jnp.square(jnp.maximum(u, 0.0)).astype(_BF16)
    o_ref[...] += _dot(r, wd_ref[...])

    if final_norm:
        @pl.when(pl.program_id(1) == pl.num_programs(1) - 1)
        def _finish():
            o_ref[...] = _rms(o_ref[...], gf_ref[...])


def _mlp(x, g, w_up, w_down, layer, *, tm, tf, g_final=None):
    m, d = x.shape
    f = w_up.shape[2]
    assert m % tm == 0 and f % tf == 0
    operands = [x, g.reshape(1, d), w_up, w_down]
    in_specs = [
        pl.BlockSpec((tm, d), lambda i, j: (i, 0)),
        pl.BlockSpec((1, d), lambda i, j: (0, 0)),
        pl.BlockSpec((None, d, tf), lambda i, j: (layer, 0, j)),
        pl.BlockSpec((None, tf, d), lambda i, j: (layer, j, 0)),
    ]
    if g_final is not None:
        operands.append(g_final.reshape(1, d))
        in_specs.append(pl.BlockSpec((1, d), lambda i, j: (0, 0)))
    return pl.pallas_call(
        functools.partial(_mlp_kernel, final_norm=g_final is not None),
        grid=(m // tm, f // tf),
        in_specs=in_specs,
        out_specs=pl.BlockSpec((tm, d), lambda i, j: (i, 0)),
        out_shape=jax.ShapeDtypeStruct((m, d), _F32),
        scratch_shapes=[pltpu.VMEM((tm, d), _BF16)],
        compiler_params=_params("parallel", "arbitrary"),
    )(*operands)


def _moba_kernel(q_ref, k_ref, v_ref, o_ref, km_ref, onehot_ref, *, n_blk):
    qi = pl.program_id(2)
    blk = MOBA_BLOCK
    kb = BLOCKS_PER_STEP * blk
    heads = q_ref.shape[2] // LANES

    @pl.when(qi == 0)
    def _prepare():
        km_ref[...] = jnp.zeros_like(km_ref)
        for h in range(heads):
            for j in range(n_blk):
                rows = k_ref[0, j * blk:(j + 1) * blk, h * LANES:(h + 1) * LANES].astype(_F32)
                km_ref[h, j:j + 1, :] = jnp.sum(rows, axis=0, keepdims=True) * (1.0 / blk)
        key = lax.broadcasted_iota(jnp.int32, onehot_ref.shape, 0)
        lane = lax.broadcasted_iota(jnp.int32, onehot_ref.shape, 1)
        onehot_ref[...] = ((key >= lane * blk) & (key < (lane + 1) * blk)).astype(_BF16)

    tq = MOBA_Q_BLOCKS * blk
    tiles = MOBA_Q_BLOCKS // BLOCKS_PER_STEP
    row = lax.broadcasted_iota(jnp.int32, (tq, kb), 0)
    col = lax.broadcasted_iota(jnp.int32, (tq, kb), 1)
    nb8 = -(-n_blk // 8) * 8
    blk_id = lax.broadcasted_iota(jnp.int32, (nb8, tq), 0)
    own = qi * MOBA_Q_BLOCKS
    for p in range(1, MOBA_Q_BLOCKS):
        own = own + (lax.broadcasted_iota(jnp.int32, (nb8, tq), 1) >= p * blk).astype(jnp.int32)

    def masked_queries(h):
        q = q_ref[0, :, h * LANES:(h + 1) * LANES]
        km = km_ref[h]
        km_hi = km.astype(_BF16)
        km_lo = (km - km_hi.astype(_F32)).astype(_BF16)
        gate = (_dot_nt(km_hi, q) + _dot_nt(km_lo, q))[:nb8]
        g = jnp.where(blk_id < own, gate, NEG_INF)
        rank = jnp.zeros((nb8, tq), jnp.int32)
        for j in range(n_blk):
            other = g[j:j + 1, :]
            rank = rank + ((other > g) | ((other == g) & (j < blk_id))).astype(jnp.int32)
        keep = ((blk_id < own) & (rank < MOBA_TOPK)) | (blk_id == own)
        mask_t = jnp.where(keep, 0.0, NEG_INF)
        mask_t = jnp.concatenate([mask_t, jnp.full((LANES - nb8, tq), NEG_INF, _F32)], axis=0)
        return jnp.concatenate([q, mask_t.T.astype(_BF16)], axis=1)

    def key_blocks(h, sbi, q_aug, carry, diagonal):
        hs = slice(h * LANES, (h + 1) * LANES)
        start = pl.multiple_of(sbi * kb, kb)
        k_aug = jnp.concatenate([k_ref[0, pl.ds(start, kb), hs], onehot_ref[pl.ds(start, kb), :]], axis=1)
        s = _dot_nt(q_aug, k_aug)
        if diagonal:
            s = jnp.where(col + (start - qi * tq) <= row, s, NEG_INF)
        mx = jnp.max(s, axis=-1, keepdims=True)
        v_aug = jnp.concatenate([v_ref[0, pl.ds(start, kb), hs], ones], axis=1)
        if carry is None:
            return mx, _dot(jnp.exp2(s - mx).astype(_BF16), v_aug)
        m, acc = carry
        m_new = jnp.maximum(m, mx)
        return m_new, jnp.exp2(m - m_new) * acc + _dot(jnp.exp2(s - m_new).astype(_BF16), v_aug)

    ones = jnp.ones((kb, LANES), _BF16)
    q_aug = [masked_queries(h) for h in range(heads)]
    carry = tuple(None for _ in range(heads))
    for t in range(tiles):
        carry = tuple(key_blocks(h, qi * tiles + t, q_aug[h], carry[h], True) for h in range(heads))

    def past_blocks(it, carry):
        return tuple(key_blocks(h, qi * tiles - 1 - it, q_aug[h], carry[h], False) for h in range(heads))

    carry = lax.fori_loop(0, qi * tiles, past_blocks, carry)
    for h in range(heads):
        acc = carry[h][1]
        o_ref[0, :, h * LANES:(h + 1) * LANES] = acc[:, :LANES] / acc[:, LANES:]


def _head_group_specs(seq, q_rows, head0, hp):
    assert head0 % hp == 0
    w = hp * LANES
    return [
        pl.BlockSpec((1, q_rows, w), lambda b, h, i: (b, i, head0 // hp + h)),
        pl.BlockSpec((1, seq, w), lambda b, h, i: (b, 0, head0 // hp + h)),
        pl.BlockSpec((1, seq, w), lambda b, h, i: (b, 0, head0 // hp + h)),
    ]


def _moba(q, k, v, *, bsz, seq):
    n_blk = seq // MOBA_BLOCK
    hp = MOBA_HEADS_PER_STEP
    tq = MOBA_BLOCK * MOBA_Q_BLOCKS
    assert seq % tq == 0 and MOBA_TOPK < n_blk <= LANES and N_HEADS_A % hp == 0 and MOBA_Q_BLOCKS % BLOCKS_PER_STEP == 0
    return pl.pallas_call(
        functools.partial(_moba_kernel, n_blk=n_blk),
        grid=(bsz, N_HEADS_A // hp, seq // tq),
        in_specs=_head_group_specs(seq, tq, SLAB_A0, hp),
        out_specs=pl.BlockSpec((1, tq, hp * LANES), lambda b, h, i: (b, i, h)),
        out_shape=jax.ShapeDtypeStruct((bsz, seq, N_HEADS_A * HEAD_DIM), _F32),
        scratch_shapes=[pltpu.VMEM((hp, LANES, LANES), _F32), pltpu.VMEM((seq, LANES), _BF16)],
        compiler_params=_params("parallel", "parallel", "arbitrary"),
    )(q, k, v)


def _sb_kernel(q_ref, k_ref, v_ref, o_ref):
    qi = pl.program_id(2)
    blk = SB_BLOCK
    kb = SB_BLOCKS_PER_STEP * blk
    heads = q_ref.shape[2] // LANES
    row2 = lax.broadcasted_iota(jnp.int32, (2 * blk, blk), 0)
    col2 = lax.broadcasted_iota(jnp.int32, (2 * blk, blk), 1)
    later = ((row2 > col2) & (row2 < blk) | (row2 - blk > col2)).astype(_BF16)
    past = lax.broadcasted_iota(jnp.int32, (kb, kb), 1) < lax.broadcasted_iota(jnp.int32, (kb, kb), 0)

    def suffix_sums(x):
        hi = x.astype(_BF16)
        lo = (x - hi.astype(_F32)).astype(_BF16)
        return _dot(jnp.concatenate([hi, lo], axis=1), later)

    def key_blocks(h, sbi, carry, diagonal):
        hs = slice(h * LANES, (h + 1) * LANES)
        start = pl.multiple_of(sbi * kb, kb)
        z = _dot_nt(q_ref[0, :, hs], k_ref[0, pl.ds(start, kb), hs])
        neg_abs = lax.bitcast_convert_type(lax.bitcast_convert_type(z, jnp.uint32) | jnp.uint32(F32_SIGN_BIT), _F32)
        log_beta = jnp.minimum(z, 0.0) - jnp.log(1.0 + jnp.exp2(neg_abs)) * LOG2_E
        log_keep = log_beta - z
        if diagonal:
            log_keep = jnp.where(past, log_keep, 0.0)
        sticks = []
        tail = None if carry is None else carry[1]
        for p in reversed(range(SB_BLOCKS_PER_STEP)):
            lk = log_keep[:, p * blk:(p + 1) * blk]
            st = suffix_sums(lk)
            sticks.append(st if tail is None else st + tail)
            total = jnp.sum(lk, axis=-1, keepdims=True)
            tail = total if tail is None else tail + total
        w = jnp.exp2(log_beta + jnp.concatenate(sticks[::-1], axis=1))
        if diagonal:
            w = jnp.where(past, w, 0.0)
        out = _dot(w.astype(_BF16), v_ref[0, pl.ds(start, kb), hs])
        return (out if carry is None else carry[0] + out), tail

    def weights_alive(carry):
        worst = jnp.max(functools.reduce(jnp.maximum, [c[1] for c in carry]))
        return (worst >= EXP2_UNDERFLOW).astype(jnp.int32)

    carry0 = tuple(key_blocks(h, qi, None, True) for h in range(heads))

    def more_blocks(state):
        it, alive, _ = state
        return (it < qi) & (alive > 0)

    def past_blocks(state):
        it, _, carry = state
        carry = tuple(key_blocks(h, qi - 1 - it, carry[h], False) for h in range(heads))
        return it + 1, weights_alive(carry), carry

    _, _, carry = lax.while_loop(more_blocks, past_blocks, (jnp.int32(0), weights_alive(carry0), carry0))
    for h in range(heads):
        o_ref[0, :, h * LANES:(h + 1) * LANES] = carry[h][0]


def _stick_breaking(q, k, v, *, bsz, seq):
    hp = SB_HEADS_PER_STEP
    tq = SB_BLOCK * SB_BLOCKS_PER_STEP
    assert seq % tq == 0 and N_HEADS_B % hp == 0
    return pl.pallas_call(
        _sb_kernel,
        grid=(bsz, N_HEADS_B // hp, seq // tq),
        in_specs=_head_group_specs(seq, tq, SLAB_B0, hp),
        out_specs=pl.BlockSpec((1, tq, hp * LANES), lambda b, h, i: (b, i, h)),
        out_shape=jax.ShapeDtypeStruct((bsz, seq, N_HEADS_B * HEAD_DIM), _F32),
        compiler_params=_params("parallel", "parallel", "parallel"),
    )(q, k, v)


def _dilated_kernel(q_ref, k_ref, v_ref, o_ref, qf, kf, vf, og, lg, *, seq):
    g = pl.program_id(2)
    qf[...] = q_ref[0].astype(_F32)
    kf[...] = k_ref[0].astype(_F32)
    vf[...] = v_ref[0].astype(_F32)
    row2 = lax.broadcasted_iota(jnp.int32, (BAND, 2 * BAND), 0)
    col2 = lax.broadcasted_iota(jnp.int32, (BAND, 2 * BAND), 1)
    own_block = (col2 >= BAND) & (col2 - BAND <= row2)
    prev_block = (col2 < BAND) & (col2 >= row2)

    def rows(ref, start, d):
        return ref[pl.ds(start, BAND, stride=d), :] if d > 1 else ref[pl.ds(start, BAND), :]

    def unit(gi, d, r, nb):
        q_start = r + nb * (BAND * d)
        p_start = q_start - jnp.where(nb > 0, BAND * d, 0)
        qu = rows(qf, q_start, d).astype(_BF16)
        ku = jnp.concatenate([rows(kf, p_start, d), rows(kf, q_start, d)], axis=0).astype(_BF16)
        vu = jnp.concatenate([rows(vf, p_start, d), rows(vf, q_start, d)], axis=0).astype(_BF16)
        s = jnp.where(own_block | (prev_block & (nb > 0)), _dot_nt(qu, ku), NEG_INF)
        mx = jnp.max(s, axis=-1, keepdims=True)
        e = jnp.exp2(s - mx)
        den = jnp.sum(e, axis=-1, keepdims=True)
        out = _dot((e / den).astype(_BF16), vu)
        lse = mx + jnp.log(den) * LOG2_E
        dst = pl.ds(q_start, BAND, stride=d) if d > 1 else pl.ds(q_start, BAND)
        og[gi, dst, :] = out
        lg[gi, dst, :] = jnp.broadcast_to(lse, (BAND, LANES))

    def pattern(gi, d):
        per_step = min(DILATED_UNITS_PER_STEP, seq // BAND)

        def unit_group(i, _):
            for u in range(per_step):
                idx = i * per_step + u
                if d == 1:
                    unit(gi, d, 0, idx)
                else:
                    unit(gi, d, lax.rem(idx, d), lax.div(idx, d))
            return 0

        lax.fori_loop(0, seq // (BAND * per_step), unit_group, 0)

    for gi, d in enumerate(DILATIONS):
        pl.when(g == gi)(functools.partial(pattern, gi, d))

    @pl.when(g == len(DILATIONS) - 1)
    def _mix():
        def mix_rows(c, _):
            sl = pl.ds(pl.multiple_of(c * MIX_ROWS, MIX_ROWS), MIX_ROWS)
            l = [lg[gi, sl, :] for gi in range(len(DILATIONS))]
            mx = functools.reduce(jnp.maximum, l)
            e = [jnp.exp2(li - mx) for li in l]
            den = functools.reduce(jnp.add, e)
            o_ref[0, sl, :] = functools.reduce(jnp.add, [(ei / den) * og[gi, sl, :] for gi, ei in enumerate(e)])
            return 0

        lax.fori_loop(0, seq // MIX_ROWS, mix_rows, 0)


def _dilated(q, k, v, *, bsz, seq):
    assert seq % (BAND * min(DILATED_UNITS_PER_STEP, seq // BAND)) == 0
    assert seq % (BAND * max(DILATIONS)) == 0 and seq % MIX_ROWS == 0
    n_pat = len(DILATIONS)
    per = C_HEADS_PER_PATTERN
    spec = pl.BlockSpec((1, seq, LANES), lambda b, j, g: (b, 0, SLAB_C0 + g * per + j))
    return pl.pallas_call(
        functools.partial(_dilated_kernel, seq=seq),
        grid=(bsz, per, n_pat),
        in_specs=[spec, spec, spec],
        out_specs=pl.BlockSpec((1, seq, LANES), lambda b, j, g: (b, 0, j)),
        out_shape=jax.ShapeDtypeStruct((bsz, seq, per * HEAD_DIM), _F32),
        scratch_shapes=[pltpu.VMEM((seq, LANES), _F32)] * 3 + [pltpu.VMEM((n_pat, seq, LANES), _F32)] * 2,
        compiler_params=_params("parallel", "parallel", "arbitrary"),
    )(q, k, v)


def _rope_tables(seq):
    inv_freq = 1.0 / (ROPE_THETA ** (jnp.arange(0, HEAD_DIM, 2, dtype=_F32) / HEAD_DIM))
    ang = jnp.arange(seq, dtype=_F32)[:, None] * inv_freq[None, :]
    cos, sin = jnp.cos(ang), jnp.sin(ang)
    return jnp.concatenate([cos, cos], axis=-1), jnp.concatenate([-sin, sin], axis=-1)


def kernel(x, mem, g_mix, w_in, g_out_a, g_out_b, g_out_c, w_out, g_cross, g_mem,
           w_cq, w_ckv, w_co, g_mlp, w_up, w_down, g_final):
    bsz, seq, d = x.shape
    depth = w_in.shape[0]
    mem_len = mem.shape[1]
    m = bsz * seq
    tm_in, tm_out, tm_mlp = (min(rows, seq) for rows in (IN_PROJ_ROWS, OUT_CROSS_ROWS, MLP_ROWS))
    width = N_MIX_HEADS * HEAD_DIM
    cos2, sin2 = _rope_tables(seq)
    w_in_b, w_out_b, w_cq_b, w_ckv_b, w_co_b, w_up_b, w_down_b = (
        w.astype(_BF16) for w in (w_in, w_out, w_cq, w_ckv, w_co, w_up, w_down))
    g_out = jnp.concatenate([g_out_a, g_out_b, g_out_c], axis=-1)

    xf = x.reshape(m, d)
    memf = mem.reshape(bsz * mem_len, d)
    for l in range(depth):
        q = _in_proj(xf, g_mix[l], w_in_b, l, 0, tm=tm_in, seq=seq, scale=QK_SCALE,
                     rope=(cos2 * QK_SCALE, sin2 * QK_SCALE))
        k = _in_proj(xf, g_mix[l], w_in_b, l, 1, tm=tm_in, seq=seq, rope=(cos2, sin2))
        v = _in_proj(xf, g_mix[l], w_in_b, l, 2, tm=tm_in, seq=seq)
        q, k, v = (a.reshape(bsz, seq, width) for a in (q, k, v))
        parts = [mix(q, k, v, bsz=bsz, seq=seq).reshape(m, -1) for mix in (_moba, _stick_breaking, _dilated)]
        kv = _norm_matmul(memf, g_mem, w_ckv_b, l, tm=min(MEM_KV_ROWS, bsz * mem_len), tn=MEM_KV_COLS,
                          out_dtype=_BF16).reshape(bsz, mem_len, -1)
        xf = _out_cross(parts, g_out[l], w_out_b, xf, g_cross[l], w_cq_b, kv, w_co_b, l, seq=seq, tm=tm_out)
        xf = _mlp(xf, g_mlp[l], w_up_b, w_down_b, l, tm=tm_mlp, tf=MLP_HIDDEN_TILE,
                  g_final=g_final if l == depth - 1 else None)
    return xf.reshape(bsz, seq, d)
```

```python
import functools

import jax
import jax.numpy as jnp
from jax import lax
from jax.experimental import pallas as pl
from jax.experimental.pallas import tpu as pltpu

HEAD_DIM = 128
N_MIX_HEADS = 16
N_HEADS_A = 4
N_HEADS_B = 6
N_HEADS_C = 6
DILATIONS = (1, 4, 16)
BAND = 128
C_HEADS_PER_PATTERN = N_HEADS_C // len(DILATIONS)
MOBA_BLOCK = 256
MOBA_TOPK = 3
SB_BLOCK = 256
BLOCKS_PER_STEP = 2
MOBA_Q_BLOCKS = 2
MOBA_HEADS_PER_STEP = 4
MOBA_FIXED_SHIFT_MARGIN = 60.0
SB_BLOCKS_PER_STEP = 2
SB_HEADS_PER_STEP = 6
DILATED_UNITS_PER_STEP = 32
ROPE_THETA = 10000.0
CROSS_HEADS = 4
CROSS_HEAD_DIM = 128
RMS_EPS = 1e-6
NEG_INF = -1e30
EXP2_UNDERFLOW = -150.0
LOG2_E = 1.4426950408889634
F32_SIGN_BIT = 0x80000000
QK_SCALE = HEAD_DIM ** -0.5 * LOG2_E

LANES = 128
V7X_VMEM_BYTES = 64 * 1024 * 1024
V7X_VMEM_LIMIT_BYTES = V7X_VMEM_BYTES * 7 // 8

IN_PROJ_ROWS = 1024
OUT_CROSS_ROWS = 512
MLP_ROWS = 1024
MLP_HIDDEN_TILE = 512
MEM_KV_ROWS = 512
MEM_KV_COLS = 512
MIX_ROWS = 256

_BF16 = jnp.bfloat16
_F32 = jnp.float32

HEAD_A0, HEAD_B0, HEAD_C0 = 0, N_HEADS_A, N_HEADS_A + N_HEADS_B
SLAB_B0, SLAB_C0, SLAB_A0 = 0, N_HEADS_B, N_HEADS_B + N_HEADS_C


def _slab_of_head(h):
    if h < HEAD_B0:
        return SLAB_A0 + h - HEAD_A0
    return SLAB_B0 + h - HEAD_B0 if h < HEAD_C0 else SLAB_C0 + h - HEAD_C0


def _dot(a, b):
    return jnp.dot(a, b, preferred_element_type=_F32)


def _dot_nt(a, b):
    return lax.dot_general(a, b, (((1,), (1,)), ((), ())), preferred_element_type=_F32)


def _rms(x, g):
    ms = jnp.mean(x * x, axis=-1, keepdims=True)
    return x * lax.rsqrt(ms + RMS_EPS) * g


def _params(*semantics):
    return pltpu.CompilerParams(dimension_semantics=semantics, vmem_limit_bytes=V7X_VMEM_LIMIT_BYTES)


def _norm_matmul_kernel(x_ref, g_ref, w_ref, o_ref, hn_ref):
    @pl.when(pl.program_id(1) == 0)
    def _normalise():
        hn_ref[...] = _rms(x_ref[...], g_ref[...]).astype(_BF16)

    o_ref[...] = _dot(hn_ref[...], w_ref[...]).astype(o_ref.dtype)


def _norm_matmul(x, g, w, layer, *, tm, tn, out_dtype):
    m, k = x.shape
    n = w.shape[2]
    assert m % tm == 0 and n % tn == 0
    return pl.pallas_call(
        _norm_matmul_kernel,
        grid=(m // tm, n // tn),
        in_specs=[
            pl.BlockSpec((tm, k), lambda i, j: (i, 0)),
            pl.BlockSpec((1, k), lambda i, j: (0, 0)),
            pl.BlockSpec((None, k, tn), lambda i, j: (layer, 0, j)),
        ],
        out_specs=pl.BlockSpec((tm, tn), lambda i, j: (i, j)),
        out_shape=jax.ShapeDtypeStruct((m, n), out_dtype),
        scratch_shapes=[pltpu.VMEM((tm, k), _BF16)],
        compiler_params=_params("parallel", "arbitrary"),
    )(x, g.reshape(1, k), w)


def _in_proj_kernel(*refs, scale, rotary_heads):
    if any(rotary_heads):
        x_ref, g_ref, w_ref, cos_ref, sin_ref, o_ref = refs
        c, s = cos_ref[...], sin_ref[...]
    else:
        x_ref, g_ref, w_ref, o_ref = refs
    acc = _dot(_rms(x_ref[...], g_ref[...]).astype(_BF16), w_ref[...])
    for h, rotary in enumerate(rotary_heads):
        a = acc[:, h * LANES:(h + 1) * LANES]
        if rotary:
            a = a * c + pltpu.roll(a, LANES // 2, 1) * s
        elif scale != 1.0:
            a = a * scale
        dst = _slab_of_head(h)
        o_ref[:, dst * LANES:(dst + 1) * LANES] = a.astype(o_ref.dtype)


def _in_proj(x, g, w, layer, part, *, tm, seq, scale=1.0, rope=None):
    m, k = x.shape
    n = N_MIX_HEADS * HEAD_DIM
    assert m % tm == 0 and seq % tm == 0 and w.shape[2] == 3 * n
    rotary_heads = tuple(rope is not None and not HEAD_B0 <= h < HEAD_C0 for h in range(N_MIX_HEADS))
    tiles_per_seq = seq // tm
    operands = [x, g.reshape(1, k), w]
    in_specs = [
        pl.BlockSpec((tm, k), lambda i: (i, 0)),
        pl.BlockSpec((1, k), lambda i: (0, 0)),
        pl.BlockSpec((None, k, n), lambda i: (layer, 0, part)),
    ]
    if rope is not None:
        operands += list(rope)
        in_specs += [pl.BlockSpec((tm, LANES), lambda i: (i % tiles_per_seq, 0))] * 2
    return pl.pallas_call(
        functools.partial(_in_proj_kernel, scale=scale, rotary_heads=rotary_heads),
        grid=(m // tm,),
        in_specs=in_specs,
        out_specs=pl.BlockSpec((tm, n), lambda i: (i, 0)),
        out_shape=jax.ShapeDtypeStruct((m, n), _BF16),
        compiler_params=_params("parallel"),
    )(*operands)


def _out_cross_kernel(a_ref, b_ref, c_ref, go_ref, wout_ref, x_ref, gc_ref, wq_ref, kv_ref, wo_ref, o_ref):
    x = x_ref[...]
    off = 0
    for ref in (a_ref, b_ref, c_ref):
        width = ref.shape[1]
        hn = _rms(ref[...], go_ref[:, off:off + width]).astype(_BF16)
        x = x + _dot(hn, wout_ref[off:off + width, :])
        off += width
    q = _dot(_rms(x, gc_ref[...]).astype(_BF16), wq_ref[...]).astype(_BF16)
    kv = kv_ref[0]
    width = CROSS_HEADS * CROSS_HEAD_DIM
    outs = []
    for h in range(CROSS_HEADS):
        sl = slice(h * CROSS_HEAD_DIM, (h + 1) * CROSS_HEAD_DIM)
        s = _dot_nt(q[:, sl], kv[:, sl]) * (CROSS_HEAD_DIM ** -0.5)
        e = jnp.exp(s - jnp.max(s, axis=-1, keepdims=True))
        p = e / jnp.sum(e, axis=-1, keepdims=True)
        outs.append(_dot(p.astype(_BF16), kv[:, width + h * CROSS_HEAD_DIM: width + (h + 1) * CROSS_HEAD_DIM]))
    o_ref[...] = x + _dot(jnp.concatenate(outs, axis=1).astype(_BF16), wo_ref[...])


def _out_cross(parts, g_out, w_out, x, g_cross, w_q, kv, w_o, layer, *, seq, tm):
    m, d = x.shape
    widths = [p.shape[1] for p in parts]
    width = w_q.shape[2]
    mem_len = kv.shape[1]
    assert sum(widths) == w_out.shape[1] and m % tm == 0 and seq % tm == 0
    tiles_per_seq = seq // tm
    return pl.pallas_call(
        _out_cross_kernel,
        grid=(m // tm,),
        in_specs=[pl.BlockSpec((tm, wd), lambda i: (i, 0)) for wd in widths] + [
            pl.BlockSpec((1, w_out.shape[1]), lambda i: (0, 0)),
            pl.BlockSpec((None,) + w_out.shape[1:], lambda i: (layer, 0, 0)),
            pl.BlockSpec((tm, d), lambda i: (i, 0)),
            pl.BlockSpec((1, d), lambda i: (0, 0)),
            pl.BlockSpec((None, d, width), lambda i: (layer, 0, 0)),
            pl.BlockSpec((1, mem_len, 2 * width), lambda i: (i // tiles_per_seq, 0, 0)),
            pl.BlockSpec((None, width, d), lambda i: (layer, 0, 0)),
        ],
        out_specs=pl.BlockSpec((tm, d), lambda i: (i, 0)),
        out_shape=jax.ShapeDtypeStruct((m, d), _F32),
        compiler_params=_params("parallel"),
    )(*parts, g_out.reshape(1, -1), w_out, x, g_cross.reshape(1, d), w_q, kv, w_o)


def _mlp_kernel(*refs, final_norm):
    if final_norm:
        x_ref, g_ref, wu_ref, wd_ref, gf_ref, o_ref, hn_ref = refs
    else:
        x_ref, g_ref, wu_ref, wd_ref, o_ref, hn_ref = refs

    @pl.when(pl.program_id(1) == 0)
    def _start():
        x = x_ref[...]
        hn_ref[...] = _rms(x, g_ref[...]).astype(_BF16)
        o_ref[...] = x

    u = _dot(hn_ref[...], wu_ref[...])
    r = jnp.square(jnp.maximum(u, 0.0)).astype(_BF16)
    o_ref[...] += _dot(r, wd_ref[...])

    if final_norm:
        @pl.when(pl.program_id(1) == pl.num_programs(1) - 1)
        def _finish():
            o_ref[...] = _rms(o_ref[...], gf_ref[...])


def _mlp(x, g, w_up, w_down, layer, *, tm, tf, g_final=None):
    m, d = x.shape
    f = w_up.shape[2]
    assert m % tm == 0 and f % tf == 0
    operands = [x, g.reshape(1, d), w_up, w_down]
    in_specs = [
        pl.BlockSpec((tm, d), lambda i, j: (i, 0)),
        pl.BlockSpec((1, d), lambda i, j: (0, 0)),
        pl.BlockSpec((None, d, tf), lambda i, j: (layer, 0, j)),
        pl.BlockSpec((None, tf, d), lambda i, j: (layer, j, 0)),
    ]
    if g_final is not None:
        operands.append(g_final.reshape(1, d))
        in_specs.append(pl.BlockSpec((1, d), lambda i, j: (0, 0)))
    return pl.pallas_call(
        functools.partial(_mlp_kernel, final_norm=g_final is not None),
        grid=(m // tm, f // tf),
        in_specs=in_specs,
        out_specs=pl.BlockSpec((tm, d), lambda i, j: (i, 0)),
        out_shape=jax.ShapeDtypeStruct((m, d), _F32),
        scratch_shapes=[pltpu.VMEM((tm, d), _BF16)],
        compiler_params=_params("parallel", "arbitrary"),
    )(*operands)


def _moba_kernel(q_ref, k_ref, v_ref, o_ref, km_ref, kn_ref, onehot_ref, *, n_blk):
    qi = pl.program_id(2)
    blk = MOBA_BLOCK
    kb = BLOCKS_PER_STEP * blk
    heads = q_ref.shape[2] // LANES

    @pl.when(qi == 0)
    def _prepare():
        km_ref[...] = jnp.zeros_like(km_ref)
        for h in range(heads):
            norm2 = None
            for j in range(n_blk):
                rows = k_ref[0, j * blk:(j + 1) * blk, h * LANES:(h + 1) * LANES].astype(_F32)
                km_ref[h, j:j + 1, :] = jnp.sum(rows, axis=0, keepdims=True) * (1.0 / blk)
                n2 = jnp.max(jnp.sum(rows * rows, axis=-1, keepdims=True), axis=0, keepdims=True)
                norm2 = n2 if norm2 is None else jnp.maximum(norm2, n2)
            kn_ref[h] = jnp.broadcast_to(norm2, kn_ref.shape[1:])
        key = lax.broadcasted_iota(jnp.int32, onehot_ref.shape, 0)
        lane = lax.broadcasted_iota(jnp.int32, onehot_ref.shape, 1)
        onehot_ref[...] = ((key >= lane * blk) & (key < (lane + 1) * blk)).astype(_BF16)

    tq = MOBA_Q_BLOCKS * blk
    tiles = MOBA_Q_BLOCKS // BLOCKS_PER_STEP
    row = lax.broadcasted_iota(jnp.int32, (tq, kb), 0)
    col = lax.broadcasted_iota(jnp.int32, (tq, kb), 1)
    nb8 = -(-n_blk // 8) * 8
    blk_id = lax.broadcasted_iota(jnp.int32, (nb8, tq), 0)
    own = qi * MOBA_Q_BLOCKS
    for p in range(1, MOBA_Q_BLOCKS):
        own = own + (lax.broadcasted_iota(jnp.int32, (nb8, tq), 1) >= p * blk).astype(jnp.int32)

    def masked_queries(h):
        q = q_ref[0, :, h * LANES:(h + 1) * LANES]
        km = km_ref[h]
        km_hi = km.astype(_BF16)
        km_lo = (km - km_hi.astype(_F32)).astype(_BF16)
        gate = (_dot_nt(km_hi, q) + _dot_nt(km_lo, q))[:nb8]
        g = jnp.where(blk_id < own, gate, NEG_INF)
        rank = jnp.zeros((nb8, tq), jnp.int32)
        for j in range(n_blk):
            other = g[j:j + 1, :]
            rank = rank + ((other > g) | ((other == g) & (j < blk_id))).astype(jnp.int32)
        keep = ((blk_id < own) & (rank < MOBA_TOPK)) | (blk_id == own)
        mask_t = jnp.where(keep, 0.0, NEG_INF)
        mask_t = jnp.concatenate([mask_t, jnp.full((LANES - nb8, tq), NEG_INF, _F32)], axis=0)
        return jnp.concatenate([q, mask_t.T.astype(_BF16)], axis=1)

    def key_blocks(h, sbi, q_aug, carry, diagonal, fixed_shift=False):
        hs = slice(h * LANES, (h + 1) * LANES)
        start = pl.multiple_of(sbi * kb, kb)
        k_aug = jnp.concatenate([k_ref[0, pl.ds(start, kb), hs], onehot_ref[pl.ds(start, kb), :]], axis=1)
        s = _dot_nt(q_aug, k_aug)
        if diagonal:
            s = jnp.where(col + (start - qi * tq) <= row, s, NEG_INF)
        v_aug = jnp.concatenate([v_ref[0, pl.ds(start, kb), hs], ones], axis=1)
        if fixed_shift:
            m, acc = carry
            return m, acc + _dot(jnp.exp2(s - m).astype(_BF16), v_aug)
        mx = jnp.max(s, axis=-1, keepdims=True)
        if carry is None:
            return mx, _dot(jnp.exp2(s - mx).astype(_BF16), v_aug)
        m, acc = carry
        m_new = jnp.maximum(m, mx)
        return m_new, jnp.exp2(m - m_new) * acc + _dot(jnp.exp2(s - m_new).astype(_BF16), v_aug)

    ones = jnp.ones((kb, LANES), _BF16)
    q_aug = [masked_queries(h) for h in range(heads)]
    carry = tuple(None for _ in range(heads))
    for t in range(tiles):
        carry = tuple(key_blocks(h, qi * tiles + t, q_aug[h], carry[h], True) for h in range(heads))

    headroom = []
    for h in range(heads):
        q32 = q_ref[0, :, h * LANES:(h + 1) * LANES].astype(_F32)
        bound = jnp.sqrt(jnp.sum(q32 * q32, axis=-1, keepdims=True) * kn_ref[h][0:1, 0:1])
        headroom.append(jnp.max(bound - carry[h][0]))
    shift_is_safe = functools.reduce(jnp.maximum, headroom) <= MOBA_FIXED_SHIFT_MARGIN

    def past_tiles(fixed_shift):
        def body(it, carry):
            return tuple(key_blocks(h, qi * tiles - 1 - it, q_aug[h], carry[h], False, fixed_shift) for h in range(heads))

        return lax.fori_loop(0, qi * tiles, body, carry)

    carry = lax.cond(shift_is_safe, lambda: past_tiles(True), lambda: past_tiles(False))
    for h in range(heads):
        acc = carry[h][1]
        o_ref[0, :, h * LANES:(h + 1) * LANES] = acc[:, :LANES] / acc[:, LANES:]


def _head_group_specs(seq, q_rows, head0, hp):
    assert head0 % hp == 0
    w = hp * LANES
    return [
        pl.BlockSpec((1, q_rows, w), lambda b, h, i: (b, i, head0 // hp + h)),
        pl.BlockSpec((1, seq, w), lambda b, h, i: (b, 0, head0 // hp + h)),
        pl.BlockSpec((1, seq, w), lambda b, h, i: (b, 0, head0 // hp + h)),
    ]


def _moba(q, k, v, *, bsz, seq):
    n_blk = seq // MOBA_BLOCK
    hp = MOBA_HEADS_PER_STEP
    tq = MOBA_BLOCK * MOBA_Q_BLOCKS
    assert seq % tq == 0 and MOBA_TOPK < n_blk <= LANES and N_HEADS_A % hp == 0 and MOBA_Q_BLOCKS % BLOCKS_PER_STEP == 0
    return pl.pallas_call(
        functools.partial(_moba_kernel, n_blk=n_blk),
        grid=(bsz, N_HEADS_A // hp, seq // tq),
        in_specs=_head_group_specs(seq, tq, SLAB_A0, hp),
        out_specs=pl.BlockSpec((1, tq, hp * LANES), lambda b, h, i: (b, i, h)),
        out_shape=jax.ShapeDtypeStruct((bsz, seq, N_HEADS_A * HEAD_DIM), _F32),
        scratch_shapes=[pltpu.VMEM((hp, LANES, LANES), _F32), pltpu.VMEM((hp, 8, LANES), _F32),
                        pltpu.VMEM((seq, LANES), _BF16)],
        compiler_params=_params("parallel", "parallel", "arbitrary"),
    )(q, k, v)


def _sb_kernel(q_ref, k_ref, v_ref, o_ref):
    qi = pl.program_id(2)
    blk = SB_BLOCK
    kb = SB_BLOCKS_PER_STEP * blk
    heads = q_ref.shape[2] // LANES
    row2 = lax.broadcasted_iota(jnp.int32, (2 * blk, blk), 0)
    col2 = lax.broadcasted_iota(jnp.int32, (2 * blk, blk), 1)
    later = ((row2 > col2) & (row2 < blk) | (row2 - blk > col2)).astype(_BF16)
    past = lax.broadcasted_iota(jnp.int32, (kb, kb), 1) < lax.broadcasted_iota(jnp.int32, (kb, kb), 0)

    def suffix_sums(x):
        hi = x.astype(_BF16)
        lo = (x - hi.astype(_F32)).astype(_BF16)
        return _dot(jnp.concatenate([hi, lo], axis=1), later)

    def key_blocks(h, n_rows, start, n_blocks, carry, diagonal=False):
        hs = slice(h * LANES, (h + 1) * LANES)
        z = _dot_nt(q_ref[0, :n_rows, hs], k_ref[0, pl.ds(start, n_blocks * blk), hs])
        neg_abs = lax.bitcast_convert_type(lax.bitcast_convert_type(z, jnp.uint32) | jnp.uint32(F32_SIGN_BIT), _F32)
        log_beta = jnp.minimum(z, 0.0) - jnp.log(1.0 + jnp.exp2(neg_abs)) * LOG2_E
        log_keep = log_beta - z
        if diagonal:
            log_keep = jnp.where(past, log_keep, 0.0)
        sticks = []
        tail = None if carry is None else carry[1]
        for p in reversed(range(n_blocks)):
            lk = log_keep[:, p * blk:(p + 1) * blk]
            st = suffix_sums(lk)
            sticks.append(st if tail is None else st + tail)
            total = jnp.sum(lk, axis=-1, keepdims=True)
            tail = total if tail is None else tail + total
        w = jnp.exp2(log_beta + (sticks[0] if n_blocks == 1 else jnp.concatenate(sticks[::-1], axis=1)))
        if diagonal:
            w = jnp.where(past, w, 0.0)
        out = _dot(w.astype(_BF16), v_ref[0, pl.ds(start, n_blocks * blk), hs])
        return (out if carry is None else carry[0] + out), tail

    def weights_alive(tails):
        worst = jnp.max(functools.reduce(jnp.maximum, tails))
        return (worst >= EXP2_UNDERFLOW).astype(jnp.int32)

    def walk(n_rows, n_blocks, carry):
        n_steps = qi * (SB_BLOCKS_PER_STEP // n_blocks)

        def more(state):
            it, alive, _ = state
            return (it < n_steps) & (alive > 0)

        def step(state):
            it, _, carry = state
            start = pl.multiple_of((n_steps - 1 - it) * (n_blocks * blk), n_blocks * blk)
            carry = tuple(key_blocks(h, n_rows, start, n_blocks, carry[h]) for h in range(heads))
            return it + 1, weights_alive([c[1] for c in carry]), carry

        return lax.while_loop(more, step, (jnp.int32(0), weights_alive([c[1] for c in carry]), carry))[2]

    start0 = pl.multiple_of(qi * kb, kb)
    carry0 = tuple(key_blocks(h, kb, start0, SB_BLOCKS_PER_STEP, None, True) for h in range(heads))

    def first_block_walk():
        top = walk(blk, 1, tuple((acc[:blk], tail[:blk]) for acc, tail in carry0))
        return tuple((jnp.concatenate([top[h][0], carry0[h][0][blk:]], axis=0),
                      jnp.concatenate([top[h][1], carry0[h][1][blk:]], axis=0)) for h in range(heads))

    rest_alive = weights_alive([tail[blk:] for _, tail in carry0])
    carry = lax.cond(rest_alive > 0, lambda: walk(kb, SB_BLOCKS_PER_STEP, carry0), first_block_walk)
    for h in range(heads):
        o_ref[0, :, h * LANES:(h + 1) * LANES] = carry[h][0]


def _stick_breaking(q, k, v, *, bsz, seq):
    hp = SB_HEADS_PER_STEP
    tq = SB_BLOCK * SB_BLOCKS_PER_STEP
    assert seq % tq == 0 and N_HEADS_B % hp == 0
    return pl.pallas_call(
        _sb_kernel,
        grid=(bsz, N_HEADS_B // hp, seq // tq),
        in_specs=_head_group_specs(seq, tq, SLAB_B0, hp),
        out_specs=pl.BlockSpec((1, tq, hp * LANES), lambda b, h, i: (b, i, h)),
        out_shape=jax.ShapeDtypeStruct((bsz, seq, N_HEADS_B * HEAD_DIM), _F32),
        compiler_params=_params("parallel", "parallel", "parallel"),
    )(q, k, v)


def _dilated_kernel(q_ref, k_ref, v_ref, o_ref, qf, kf, vf, og, lg, *, seq):
    g = pl.program_id(2)
    qf[...] = q_ref[0].astype(_F32)
    kf[...] = k_ref[0].astype(_F32)
    vf[...] = v_ref[0].astype(_F32)
    row2 = lax.broadcasted_iota(jnp.int32, (BAND, 2 * BAND), 0)
    col2 = lax.broadcasted_iota(jnp.int32, (BAND, 2 * BAND), 1)
    own_block = (col2 >= BAND) & (col2 - BAND <= row2)
    prev_block = (col2 < BAND) & (col2 >= row2)

    def rows(ref, start, d):
        return ref[pl.ds(start, BAND, stride=d), :] if d > 1 else ref[pl.ds(start, BAND), :]

    def unit(gi, d, r, nb):
        q_start = r + nb * (BAND * d)
        p_start = q_start - jnp.where(nb > 0, BAND * d, 0)
        qu = rows(qf, q_start, d).astype(_BF16)
        ku = jnp.concatenate([rows(kf, p_start, d), rows(kf, q_start, d)], axis=0).astype(_BF16)
        vu = jnp.concatenate([rows(vf, p_start, d), rows(vf, q_start, d)], axis=0).astype(_BF16)
        s = jnp.where(own_block | (prev_block & (nb > 0)), _dot_nt(qu, ku), NEG_INF)
        mx = jnp.max(s, axis=-1, keepdims=True)
        e = jnp.exp2(s - mx)
        den = jnp.sum(e, axis=-1, keepdims=True)
        out = _dot((e / den).astype(_BF16), vu)
        lse = mx + jnp.log(den) * LOG2_E
        dst = pl.ds(q_start, BAND, stride=d) if d > 1 else pl.ds(q_start, BAND)
        og[gi, dst, :] = out
        lg[gi, dst, :] = jnp.broadcast_to(lse, (BAND, LANES))

    def pattern(gi, d):
        per_step = min(DILATED_UNITS_PER_STEP, seq // BAND)

        def unit_group(i, _):
            for u in range(per_step):
                idx = i * per_step + u
                if d == 1:
                    unit(gi, d, 0, idx)
                else:
                    unit(gi, d, lax.rem(idx, d), lax.div(idx, d))
            return 0

        lax.fori_loop(0, seq // (BAND * per_step), unit_group, 0)

    for gi, d in enumerate(DILATIONS):
        pl.when(g == gi)(functools.partial(pattern, gi, d))

    @pl.when(g == len(DILATIONS) - 1)
    def _mix():
        def mix_rows(c, _):
            sl = pl.ds(pl.multiple_of(c * MIX_ROWS, MIX_ROWS), MIX_ROWS)
            l = [lg[gi, sl, :] for gi in range(len(DILATIONS))]
            mx = functools.reduce(jnp.maximum, l)
            e = [jnp.exp2(li - mx) for li in l]
            den = functools.reduce(jnp.add, e)
            o_ref[0, sl, :] = functools.reduce(jnp.add, [(ei / den) * og[gi, sl, :] for gi, ei in enumerate(e)])
            return 0

        lax.fori_loop(0, seq // MIX_ROWS, mix_rows, 0)


def _dilated(q, k, v, *, bsz, seq):
    assert seq % (BAND * min(DILATED_UNITS_PER_STEP, seq // BAND)) == 0
    assert seq % (BAND * max(DILATIONS)) == 0 and seq % MIX_ROWS == 0
    n_pat = len(DILATIONS)
    per = C_HEADS_PER_PATTERN
    spec = pl.BlockSpec((1, seq, LANES), lambda b, j, g: (b, 0, SLAB_C0 + g * per + j))
    return pl.pallas_call(
        functools.partial(_dilated_kernel, seq=seq),
        grid=(bsz, per, n_pat),
        in_specs=[spec, spec, spec],
        out_specs=pl.BlockSpec((1, seq, LANES), lambda b, j, g: (b, 0, j)),
        out_shape=jax.ShapeDtypeStruct((bsz, seq, per * HEAD_DIM), _F32),
        scratch_shapes=[pltpu.VMEM((seq, LANES), _F32)] * 3 + [pltpu.VMEM((n_pat, seq, LANES), _F32)] * 2,
        compiler_params=_params("parallel", "parallel", "arbitrary"),
    )(q, k, v)


def _rope_tables(seq):
    inv_freq = 1.0 / (ROPE_THETA ** (jnp.arange(0, HEAD_DIM, 2, dtype=_F32) / HEAD_DIM))
    ang = jnp.arange(seq, dtype=_F32)[:, None] * inv_freq[None, :]
    cos, sin = jnp.cos(ang), jnp.sin(ang)
    return jnp.concatenate([cos, cos], axis=-1), jnp.concatenate([-sin, sin], axis=-1)


def kernel(x, mem, g_mix, w_in, g_out_a, g_out_b, g_out_c, w_out, g_cross, g_mem,
           w_cq, w_ckv, w_co, g_mlp, w_up, w_down, g_final):
    bsz, seq, d = x.shape
    depth = w_in.shape[0]
    mem_len = mem.shape[1]
    m = bsz * seq
    tm_in, tm_out, tm_mlp = (min(rows, seq) for rows in (IN_PROJ_ROWS, OUT_CROSS_ROWS, MLP_ROWS))
    width = N_MIX_HEADS * HEAD_DIM
    cos2, sin2 = _rope_tables(seq)
    w_in_b, w_out_b, w_cq_b, w_ckv_b, w_co_b, w_up_b, w_down_b = (
        w.astype(_BF16) for w in (w_in, w_out, w_cq, w_ckv, w_co, w_up, w_down))
    g_out = jnp.concatenate([g_out_a, g_out_b, g_out_c], axis=-1)

    xf = x.reshape(m, d)
    memf = mem.reshape(bsz * mem_len, d)
    for l in range(depth):
        q = _in_proj(xf, g_mix[l], w_in_b, l, 0, tm=tm_in, seq=seq, scale=QK_SCALE,
                     rope=(cos2 * QK_SCALE, sin2 * QK_SCALE))
        k = _in_proj(xf, g_mix[l], w_in_b, l, 1, tm=tm_in, seq=seq, rope=(cos2, sin2))
        v = _in_proj(xf, g_mix[l], w_in_b, l, 2, tm=tm_in, seq=seq)
        q, k, v = (a.reshape(bsz, seq, width) for a in (q, k, v))
        parts = [mix(q, k, v, bsz=bsz, seq=seq).reshape(m, -1) for mix in (_moba, _stick_breaking, _dilated)]
        kv = _norm_matmul(memf, g_mem, w_ckv_b, l, tm=min(MEM_KV_ROWS, bsz * mem_len), tn=MEM_KV_COLS,
                          out_dtype=_BF16).reshape(bsz, mem_len, -1)
        xf = _out_cross(parts, g_out[l], w_out_b, xf, g_cross[l], w_cq_b, kv, w_co_b, l, seq=seq, tm=tm_out)
        xf = _mlp(xf, g_mlp[l], w_up_b, w_down_b, l, tm=tm_mlp, tf=MLP_HIDDEN_TILE,
                  g_final=g_final if l == depth - 1 else None)
    return xf.reshape(bsz, seq, d)
```
